```python
import jax, jax.numpy as jnp
from jax import lax
import numpy as np

D_MODEL = 2048
BATCH = 8
SEQ = 2048
DEPTH = 1

CHUNK = 64
SUB_CHUNK = 16
N_SUB = CHUNK // SUB_CHUNK

A_HEADS = 8
A_HEAD_DIM = 128
A_WIDTH = A_HEADS * A_HEAD_DIM
B_HEADS = 4
B_KEY_DIM = 128
B_VAL_DIM = 256
B_KEY_WIDTH = B_HEADS * B_KEY_DIM
B_VAL_WIDTH = B_HEADS * B_VAL_DIM
GK_RANK = 16
GATE_LOGIT_NORMALIZER = 16.0
IN_WIDTH = 4 * A_WIDTH + 2 * B_KEY_WIDTH + 2 * B_VAL_WIDTH + GK_RANK + 2 * D_MODEL
FFN_HIDDEN = -(-(8 * D_MODEL) // (3 * 256)) * 256
N_MOD = 6
EPS = 1e-6

kernel_name = "hgrn2_gla_parallel_hybrid_adaln"


def _in_split_points():
    sizes = [A_WIDTH] * 4 + [B_KEY_WIDTH] * 2 + [B_VAL_WIDTH] * 2 + [GK_RANK, D_MODEL, D_MODEL]
    return [int(v) for v in np.cumsum(sizes)[:-1]]


def rmsnorm(x, w):
    x32 = x.astype(jnp.float32)
    y = x32 * lax.rsqrt(jnp.mean(x32 * x32, axis=-1, keepdims=True) + EPS)
    return (y * w.astype(jnp.float32)).astype(x.dtype)


def to_heads(t, n_heads):
    b, s, hd = t.shape
    return t.reshape(b, s, n_heads, hd // n_heads).transpose(0, 2, 1, 3)


def gated_head_rmsnorm(o, gate, w):
    b, h, t, dv = o.shape
    y = o * lax.rsqrt(jnp.mean(o * o, axis=-1, keepdims=True) + EPS) * w.astype(jnp.float32)
    y = y.transpose(0, 2, 1, 3).reshape(b, t, h * dv)
    return (y * jax.nn.silu(gate.astype(jnp.float32))).astype(gate.dtype)


def chunked_gated_linear_recurrence(q, k, v, g):
    out_dtype = v.dtype
    q, k, v, g = (t.astype(jnp.float32) for t in (q, k, v, g))
    bsz, nh, seq, dk = q.shape
    dv = v.shape[-1]
    nc = seq // CHUNK

    def to_chunks(t):
        return t.reshape(bsz, nh, nc, CHUNK, t.shape[-1]).transpose(2, 0, 1, 3, 4)

    sub_idx = jnp.arange(CHUNK) // SUB_CHUNK
    mask_off = sub_idx[None, :] < jnp.arange(N_SUB)[:, None]
    mask_diag = jnp.tril(jnp.ones((SUB_CHUNK, SUB_CHUNK), dtype=bool))
    eye_sub = jnp.eye(N_SUB, dtype=jnp.float32)

    def step(state, inp):
        qc, kc, vc, gc = inp
        b = jnp.cumsum(gc, axis=2)
        o_inter = jnp.einsum('bhid,bhde->bhie', qc * jnp.exp(b), state)
        bs = b.reshape(bsz, nh, N_SUB, SUB_CHUNK, dk)
        qs = qc.reshape(bsz, nh, N_SUB, SUB_CHUNK, dk)
        ks = kc.reshape(bsz, nh, N_SUB, SUB_CHUNK, dk)
        ref = jnp.concatenate([jnp.zeros_like(bs[:, :, :1, -1]), bs[:, :, :-1, -1]], axis=2)
        q_off = qs * jnp.exp(bs - ref[:, :, :, None, :])
        expo_off = jnp.where(mask_off[None, None, :, :, None],
                             ref[:, :, :, None, :] - b[:, :, None, :, :], -jnp.inf)
        k_off = kc[:, :, None] * jnp.exp(expo_off)
        a_off = jnp.einsum('bhsid,bhsjd->bhsij', q_off, k_off)
        expo_d = jnp.where(mask_diag[None, None, None, :, :, None],
                           bs[:, :, :, :, None, :] - bs[:, :, :, None, :, :], -jnp.inf)
        a_diag = jnp.sum(qs[:, :, :, :, None, :] * ks[:, :, :, None, :, :] * jnp.exp(expo_d), axis=-1)
        a_diag = jnp.einsum('bhsij,sr->bhsirj', a_diag, eye_sub).reshape(bsz, nh, N_SUB, SUB_CHUNK, CHUNK)
        attn = a_off + a_diag
        o_intra = jnp.einsum('bhsij,bhje->bhsie', attn, vc).reshape(bsz, nh, CHUNK, dv)
        b_last = b[:, :, -1]
        k_dec = kc * jnp.exp(b_last[:, :, None, :] - b)
        new_state = jnp.exp(b_last)[..., None] * state + jnp.einsum('bhjd,bhje->bhde', k_dec, vc)
        return new_state, o_inter + o_intra

    state0 = jnp.zeros((bsz, nh, dk, dv), jnp.float32)
    _, o = lax.scan(step, state0, (to_chunks(q), to_chunks(k), to_chunks(v), to_chunks(g)))
    o = o.transpose(1, 2, 0, 3, 4).reshape(bsz, nh, seq, dv)
    return o.astype(out_dtype) if out_dtype != jnp.float32 else o


def hybrid_mixer(h, lower_bound, w_in, w_gk2, b_gk2, a_norm_w, b_norm_w, w_up_a, w_up_b, w_o):
    proj = h @ w_in
    (qa, fa, ia, ga, qb, kb, vb, gb, gk_low, ma, mb) = jnp.split(proj, _in_split_points(), axis=-1)
    f = lower_bound + (1.0 - lower_bound) * jax.nn.sigmoid(fa.astype(jnp.float32))
    o_a = chunked_gated_linear_recurrence(
        to_heads(jax.nn.silu(qa), A_HEADS), to_heads(1.0 - f, A_HEADS),
        to_heads(ia, A_HEADS), to_heads(jnp.log(f), A_HEADS))
    y_a = gated_head_rmsnorm(o_a.astype(jnp.float32), ga, a_norm_w)
    gk = jax.nn.log_sigmoid((gk_low @ w_gk2 + b_gk2).astype(jnp.float32)) / GATE_LOGIT_NORMALIZER
    o_b = chunked_gated_linear_recurrence(
        to_heads(qb, B_HEADS) * (B_KEY_DIM ** -0.5), to_heads(kb, B_HEADS),
        to_heads(vb, B_HEADS), to_heads(gk, B_HEADS))
    y_b = gated_head_rmsnorm(o_b.astype(jnp.float32), gb, b_norm_w)
    merged = jax.nn.sigmoid(ma) * (y_a @ w_up_a) + jax.nn.sigmoid(mb) * (y_b @ w_up_b)
    return merged @ w_o


def swiglu(h, w_ffn_in, w_ffn_out):
    gate, up = jnp.split(h @ w_ffn_in, 2, axis=-1)
    return (jax.nn.silu(gate) * up) @ w_ffn_out


def setup_inputs(seed: int = 0) -> dict:
    key = jax.random.key(seed)
    ks = jax.random.split(key, 20)
    D, L = D_MODEL, DEPTH

    def nrm(k, shape, s):
        return jax.random.normal(k, shape, jnp.float32) * s

    return {
        "x": nrm(ks[0], (BATCH, SEQ, D), 1.0),
        "c": nrm(ks[1], (BATCH, D), 1.0),
        "w_ada": nrm(ks[2], (L, D, N_MOD * D), 0.5 * D ** -0.5),
        "b_ada": nrm(ks[3], (L, N_MOD * D), 0.01),
        "norm1_w": 1.0 + nrm(ks[4], (L, D), 0.02),
        "w_in": nrm(ks[5], (L, D, IN_WIDTH), D ** -0.5),
        "w_gk2": nrm(ks[6], (L, GK_RANK, B_KEY_WIDTH), GK_RANK ** -0.5),
        "b_gk2": nrm(ks[7], (L, B_KEY_WIDTH), 0.01),
        "lb_param": nrm(ks[8], (L + 1, A_WIDTH), 0.5),
        "a_norm_w": 1.0 + nrm(ks[9], (L, A_HEAD_DIM), 0.02),
        "b_norm_w": 1.0 + nrm(ks[10], (L, B_VAL_DIM), 0.02),
        "w_up_a": nrm(ks[11], (L, A_WIDTH, D), A_WIDTH ** -0.5),
        "w_up_b": nrm(ks[12], (L, B_VAL_WIDTH, D), B_VAL_WIDTH ** -0.5),
        "w_o": nrm(ks[13], (L, D, D), D ** -0.5),
        "norm2_w": 1.0 + nrm(ks[14], (L, D), 0.02),
        "w_ffn_in": nrm(ks[15], (L, D, 2 * FFN_HIDDEN), D ** -0.5),
        "w_ffn_out": nrm(ks[16], (L, FFN_HIDDEN, D), FFN_HIDDEN ** -0.5),
        "final_norm_w": 1.0 + nrm(ks[17], (D,), 0.02),
    }


def reference(x, c, w_ada, b_ada, norm1_w, w_in, w_gk2, b_gk2, lb_param, a_norm_w, b_norm_w,
              w_up_a, w_up_b, w_o, norm2_w, w_ffn_in, w_ffn_out, final_norm_w):
    lb_all = jnp.cumsum(jax.nn.softmax(lb_param.astype(jnp.float32), axis=0), axis=0)
    cond = jax.nn.silu(c)
    for l in range(DEPTH):
        mod = (cond @ w_ada[l] + b_ada[l])[:, None, :]
        sh1, sc1, gt1, sh2, sc2, gt2 = jnp.split(mod, N_MOD, axis=-1)
        h = rmsnorm(x, norm1_w[l]) * (1.0 + sc1) + sh1
        x = x + gt1 * hybrid_mixer(h, lb_all[l], w_in[l], w_gk2[l], b_gk2[l], a_norm_w[l], b_norm_w[l],
                                   w_up_a[l], w_up_b[l], w_o[l])
        h = rmsnorm(x, norm2_w[l]) * (1.0 + sc2) + sh2
        x = x + gt2 * swiglu(h, w_ffn_in[l], w_ffn_out[l])
    return rmsnorm(x, final_norm_w)
```

```python
import functools
import math

import jax
import jax.numpy as jnp
import numpy as np
from jax import lax
from jax.experimental import pallas as pl
from jax.experimental.pallas import tpu as pltpu

F32 = jnp.float32
BF16 = jnp.bfloat16

D_MODEL = 2048
A_HEADS = 8
A_HEAD_DIM = 128
A_WIDTH = A_HEADS * A_HEAD_DIM
B_HEADS = 4
B_KEY_DIM = 128
B_VAL_DIM = 256
B_KEY_WIDTH = B_HEADS * B_KEY_DIM
B_VAL_WIDTH = B_HEADS * B_VAL_DIM
GK_RANK = 16
GATE_LOGIT_NORMALIZER = 16.0
N_MOD = 6
EPS = 1e-6

LANES = 128
SUBLANES = 8
VMEM_LIMIT = 56 * 1024 * 1024

MAIN_WIDTH = 4 * A_WIDTH + 2 * B_KEY_WIDTH + 2 * B_VAL_WIDTH + 2 * D_MODEL
COL_MA, COL_MB = 0, D_MODEL
COL_QA = 2 * D_MODEL
COL_FA, COL_IA, COL_GA = COL_QA + A_WIDTH, COL_QA + 2 * A_WIDTH, COL_QA + 3 * A_WIDTH
COL_QB = COL_QA + 4 * A_WIDTH
COL_KB = COL_QB + B_KEY_WIDTH
COL_VB = COL_KB + B_KEY_WIDTH
COL_GB = COL_VB + B_VAL_WIDTH

CHUNK = 128
TM_PROJ = 1024
TN_PROJ = 1024
TM_MERGE = 256
TM_FFN = 512
TF_FFN = 512
TN_ADA = 1024


def _sigmoid(x):
    return 1.0 / (1.0 + jnp.exp(-x))


def _rms_rows(x):
    return x * lax.rsqrt(jnp.mean(x * x, axis=-1, keepdims=True) + EPS)


def _dot(a, b):
    return jnp.dot(a, b, preferred_element_type=F32)


def _dot_nt(a, b):
    return lax.dot_general(a, b, (((1,), (1,)), ((), ())), preferred_element_type=F32)


def _dot_tn(a, b):
    return lax.dot_general(a, b, (((0,), (0,)), ((), ())), preferred_element_type=F32)


def _ada_kernel(c_ref, w_ref, b_ref, o_ref):
    c = c_ref[...]
    cond = (c * _sigmoid(c)).astype(BF16)
    o_ref[...] = _dot(cond, w_ref[...].astype(BF16)) + b_ref[...]


def _ada(c, w, b):
    bsz, d = c.shape
    n = w.shape[1]
    return pl.pallas_call(
        _ada_kernel,
        grid=(n // TN_ADA,),
        in_specs=[
            pl.BlockSpec((bsz, d), lambda j: (0, 0)),
            pl.BlockSpec((d, TN_ADA), lambda j: (0, j)),
            pl.BlockSpec((1, TN_ADA), lambda j: (0, j)),
        ],
        out_specs=pl.BlockSpec((bsz, TN_ADA), lambda j: (0, j)),
        out_shape=jax.ShapeDtypeStruct((bsz, n), F32),
        compiler_params=pltpu.CompilerParams(
            dimension_semantics=("arbitrary",), vmem_limit_bytes=VMEM_LIMIT),
        name="ada",
    )(c, w, b)


def _inproj_kernel(x_ref, mod_ref, nw_ref, w_ref, wgk_ref, o_ref, gk_ref, h_scr):
    @pl.when(pl.program_id(1) == 0)
    def _():
        y = _rms_rows(x_ref[...]) * nw_ref[...]
        h = y * (1.0 + mod_ref[0, 1:2, :]) + mod_ref[0, 0:1, :]
        hb = h.astype(BF16)
        h_scr[...] = hb
        gk_ref[...] = _dot(hb, wgk_ref[...])

    o_ref[...] = _dot(h_scr[...], w_ref[...])


def _inproj(x2, mod, norm_w, w_main, w_gk1, seq):
    m, d = x2.shape
    n = w_main.shape[1]
    tiles_per_seq = seq // TM_PROJ
    return pl.pallas_call(
        _inproj_kernel,
        grid=(m // TM_PROJ, n // TN_PROJ),
        in_specs=[
            pl.BlockSpec((TM_PROJ, d), lambda i, j: (i, 0)),
            pl.BlockSpec((1, N_MOD, d), lambda i, j: (i // tiles_per_seq, 0, 0)),
            pl.BlockSpec((1, d), lambda i, j: (0, 0)),
            pl.BlockSpec((d, TN_PROJ), lambda i, j: (0, j)),
            pl.BlockSpec((d, LANES), lambda i, j: (0, 0)),
        ],
        out_specs=[
            pl.BlockSpec((TM_PROJ, TN_PROJ), lambda i, j: (i, j)),
            pl.BlockSpec((TM_PROJ, LANES), lambda i, j: (i, 0)),
        ],
        out_shape=[
            jax.ShapeDtypeStruct((m, n), F32),
            jax.ShapeDtypeStruct((m, LANES), F32),
        ],
        scratch_shapes=[pltpu.VMEM((TM_PROJ, d), BF16)],
        compiler_params=pltpu.CompilerParams(
            dimension_semantics=("arbitrary", "arbitrary"), vmem_limit_bytes=VMEM_LIMIT),
        name="inproj",
    )(x2, mod, norm_w, w_main, w_gk1)


def _levels(c):
    out, g = [], c // 2
    while g >= 1:
        out.append(g)
        g //= 2
    return out


def _pair_level(c):
    i = lax.broadcasted_iota(jnp.int32, (c, c), 0)
    j = lax.broadcasted_iota(jnp.int32, (c, c), 1)
    x = jnp.where(j < i, i ^ j, 0).astype(F32)
    top = pltpu.bitcast(x, jnp.int32) & jnp.int32(0x7F800000)
    return pltpu.bitcast(top, F32)


def _cumsum_rows(g):
    c, dk = g.shape
    row = lax.broadcasted_iota(jnp.int32, (c, dk), 0)
    x = g
    s = 1
    while s < c:
        if s < SUBLANES:
            shifted = jnp.where(row >= s, pltpu.roll(x, s, axis=0), 0.0)
        else:
            shifted = jnp.concatenate([jnp.zeros((s, dk), F32), x[: c - s, :]], axis=0)
        x = x + shifted
        s *= 2
    return x


def _mid_rows(b, g):
    c, dk = b.shape
    if 2 * g >= SUBLANES:
        parts = []
        for s in range(0, c, 2 * g):
            parts.append(jnp.broadcast_to(b[s + g - 1:s + g, :], (2 * g, dk)))
        return jnp.concatenate(parts, axis=0) if len(parts) > 1 else parts[0]
    b3 = b.reshape(c // SUBLANES, SUBLANES, dk)
    sub = lax.broadcasted_iota(jnp.int32, b3.shape, 1)
    out = None
    for s in range(0, SUBLANES, 2 * g):
        part = jnp.broadcast_to(b3[:, s + g - 1:s + g, :], b3.shape)
        out = part if out is None else jnp.where(sub >= s, part, out)
    return out.reshape(c, dk)


def _chunk_attend(q, k, v, g, st_ref, pair_lvl):
    c, dk = q.shape
    row = lax.broadcasted_iota(jnp.int32, (c, dk), 0)
    b = _cumsum_rows(g)
    vb = v.astype(BF16)

    att = jnp.zeros((c, c), F32)
    for lv in _levels(c):
        in_second = (row & lv) != 0
        if lv == 1:
            expo = jnp.where(in_second, g, 0.0)
        else:
            d = b - _mid_rows(b, lv)
            expo = jnp.where(in_second, d, -d)
        e = jnp.exp(expo)
        s = _dot_nt((q * e).astype(BF16), (k * e).astype(BF16))
        att = jnp.where(pair_lvl == float(lv), s, att)

    o = _dot(att.astype(BF16), vb)
    o = o + jnp.sum(q * k, axis=-1, keepdims=True) * v

    st = st_ref[...]
    o = o + _dot_nt((q * jnp.exp(b)).astype(BF16), st.astype(BF16))
    b_last = b[c - 1:c, :]
    k_dec = (k * jnp.exp(b_last - b)).astype(BF16)
    st_ref[...] = st * jnp.exp(b_last) + _dot_tn(vb, k_dec)
    return o


def _gated_head_norm(o, gate, w):
    return _rms_rows(o) * w * (gate * _sigmoid(gate))


def _hgrn2_kernel(qa_ref, fa_ref, ia_ref, ga_ref, lbp_ref, nw_ref, y_ref, st_ref):
    @pl.when(pl.program_id(1) == 0)
    def _():
        st_ref[...] = jnp.zeros_like(st_ref)

    pair_lvl = _pair_level(CHUNK)
    lbp = lbp_ref[...]
    pe = jnp.exp(lbp - jnp.max(lbp, axis=0, keepdims=True))
    lb_all = pe[0:1, :] / jnp.sum(pe, axis=0, keepdims=True)
    nw = nw_ref[...]
    for h in range(A_HEADS):
        sl = slice(h * A_HEAD_DIM, (h + 1) * A_HEAD_DIM)
        lb = lb_all[:, sl]
        qa = qa_ref[:, sl]
        f = lb + (1.0 - lb) * _sigmoid(fa_ref[:, sl])
        o = _chunk_attend(qa * _sigmoid(qa), 1.0 - f, ia_ref[:, sl], jnp.log(f), st_ref.at[h], pair_lvl)
        y_ref[:, sl] = _gated_head_norm(o, ga_ref[:, sl], nw).astype(y_ref.dtype)


def _hgrn2(proj, lb_param, norm_w, bsz, seq):
    m = proj.shape[0]
    nc = seq // CHUNK

    def col(c0):
        assert c0 % A_WIDTH == 0
        return pl.BlockSpec((CHUNK, A_WIDTH), lambda b, c: (b * nc + c, c0 // A_WIDTH))

    return pl.pallas_call(
        _hgrn2_kernel,
        grid=(bsz, nc),
        in_specs=[
            col(COL_QA), col(COL_FA), col(COL_IA), col(COL_GA),
            pl.BlockSpec(lb_param.shape, lambda b, c: (0, 0)),
            pl.BlockSpec((1, A_HEAD_DIM), lambda b, c: (0, 0)),
        ],
        out_specs=pl.BlockSpec((CHUNK, A_WIDTH), lambda b, c: (b * nc + c, 0)),
        out_shape=jax.ShapeDtypeStruct((m, A_WIDTH), BF16),
        scratch_shapes=[pltpu.VMEM((A_HEADS, A_HEAD_DIM, A_HEAD_DIM), F32)],
        compiler_params=pltpu.CompilerParams(
            dimension_semantics=("arbitrary", "arbitrary"), vmem_limit_bytes=VMEM_LIMIT),
        name="hgrn2",
    )(proj, proj, proj, proj, lb_param, norm_w)


def _gla_kernel(qb_ref, kb_ref, vb_ref, gb_ref, gkl_ref, wgk_ref, bgk_ref, nw_ref, y_ref, st_ref):
    @pl.when(pl.program_id(1) == 0)
    def _():
        st_ref[...] = jnp.zeros_like(st_ref)

    pair_lvl = _pair_level(CHUNK)
    z = _dot(gkl_ref[...].astype(BF16), wgk_ref[...]) + bgk_ref[...]
    gk = (jnp.minimum(z, 0.0) - jnp.log(1.0 + jnp.exp(-jnp.abs(z)))) * (1.0 / GATE_LOGIT_NORMALIZER)
    nw = nw_ref[...]
    scale = B_KEY_DIM ** -0.5
    for h in range(B_HEADS):
        ks = slice(h * B_KEY_DIM, (h + 1) * B_KEY_DIM)
        vs = slice(h * B_VAL_DIM, (h + 1) * B_VAL_DIM)
        o = _chunk_attend(qb_ref[:, ks] * scale, kb_ref[:, ks], vb_ref[:, vs], gk[:, ks],
                          st_ref.at[h], pair_lvl)
        y_ref[:, vs] = _gated_head_norm(o, gb_ref[:, vs], nw).astype(y_ref.dtype)


def _gla(proj, gk_low, w_gk2p, b_gk2, norm_w, bsz, seq):
    m = proj.shape[0]
    nc = seq // CHUNK

    def col(c0, width):
        assert c0 % width == 0
        return pl.BlockSpec((CHUNK, width), lambda b, c: (b * nc + c, c0 // width))

    return pl.pallas_call(
        _gla_kernel,
        grid=(bsz, nc),
        in_specs=[
            col(COL_QB, B_KEY_WIDTH), col(COL_KB, B_KEY_WIDTH),
            col(COL_VB, B_VAL_WIDTH), col(COL_GB, B_VAL_WIDTH),
            pl.BlockSpec((CHUNK, LANES), lambda b, c: (b * nc + c, 0)),
            pl.BlockSpec((LANES, B_KEY_WIDTH), lambda b, c: (0, 0)),
            pl.BlockSpec((1, B_KEY_WIDTH), lambda b, c: (0, 0)),
            pl.BlockSpec((1, B_VAL_DIM), lambda b, c: (0, 0)),
        ],
        out_specs=pl.BlockSpec((CHUNK, B_VAL_WIDTH), lambda b, c: (b * nc + c, 0)),
        out_shape=jax.ShapeDtypeStruct((m, B_VAL_WIDTH), BF16),
        scratch_shapes=[pltpu.VMEM((B_HEADS, B_VAL_DIM, B_KEY_DIM), F32)],
        compiler_params=pltpu.CompilerParams(
            dimension_semantics=("arbitrary", "arbitrary"), vmem_limit_bytes=VMEM_LIMIT),
        name="gla",
    )(proj, proj, proj, proj, gk_low, w_gk2p, b_gk2, norm_w)


def _merge_kernel(ya_ref, yb_ref, ma_ref, mb_ref, x_ref, mod_ref, nw_ref,
                  wua_ref, wub_ref, wo_ref, x1_ref, h2_ref):
    ua = _dot(ya_ref[...], wua_ref[...])
    ub = _dot(yb_ref[...], wub_ref[...])
    merged = _sigmoid(ma_ref[...]) * ua + _sigmoid(mb_ref[...]) * ub
    out = _dot(merged.astype(BF16), wo_ref[...])
    x1 = x_ref[...] + mod_ref[0, 2:3, :] * out
    x1_ref[...] = x1
    y = _rms_rows(x1) * nw_ref[...]
    h2_ref[...] = (y * (1.0 + mod_ref[0, 4:5, :]) + mod_ref[0, 3:4, :]).astype(h2_ref.dtype)


def _merge(ya, yb, proj, x2, mod, norm_w, w_up_a, w_up_b, w_o, seq):
    m, d = x2.shape
    assert COL_MA % d == 0 and COL_MB % d == 0
    tiles_per_seq = seq // TM_MERGE
    row = lambda i: (i, 0)
    const = lambda i: (0, 0)
    return pl.pallas_call(
        _merge_kernel,
        grid=(m // TM_MERGE,),
        in_specs=[
            pl.BlockSpec((TM_MERGE, A_WIDTH), row),
            pl.BlockSpec((TM_MERGE, B_VAL_WIDTH), row),
            pl.BlockSpec((TM_MERGE, d), lambda i: (i, COL_MA // d)),
            pl.BlockSpec((TM_MERGE, d), lambda i: (i, COL_MB // d)),
            pl.BlockSpec((TM_MERGE, d), row),
            pl.BlockSpec((1, N_MOD, d), lambda i: (i // tiles_per_seq, 0, 0)),
            pl.BlockSpec((1, d), const),
            pl.BlockSpec(w_up_a.shape, const),
            pl.BlockSpec(w_up_b.shape, const),
            pl.BlockSpec(w_o.shape, const),
        ],
        out_specs=[pl.BlockSpec((TM_MERGE, d), row), pl.BlockSpec((TM_MERGE, d), row)],
        out_shape=[jax.ShapeDtypeStruct((m, d), F32), jax.ShapeDtypeStruct((m, d), BF16)],
        compiler_params=pltpu.CompilerParams(
            dimension_semantics=("arbitrary",), vmem_limit_bytes=VMEM_LIMIT),
        name="merge",
    )(ya, yb, proj, proj, x2, mod, norm_w, w_up_a, w_up_b, w_o)


def _ffn_kernel(h_ref, wg_ref, wu_ref, wo_ref, x1_ref, mod_ref, nw_ref, o_ref, acc_ref):
    j = pl.program_id(1)

    @pl.when(j == 0)
    def _():
        acc_ref[...] = jnp.zeros_like(acc_ref)

    h = h_ref[...]
    gate = _dot(h, wg_ref[...])
    up = _dot(h, wu_ref[...])
    act = (gate * _sigmoid(gate) * up).astype(BF16)
    acc_ref[...] += _dot(act, wo_ref[...])

    @pl.when(j == pl.num_programs(1) - 1)
    def _():
        x2 = x1_ref[...] + mod_ref[0, 5:6, :] * acc_ref[...]
        o_ref[...] = _rms_rows(x2) * nw_ref[...]


def _ffn(h2, w_in, w_out, x1, mod, norm_w, seq):
    m, d = x1.shape
    hidden = w_out.shape[0]
    nf = hidden // TF_FFN
    tiles_per_seq = seq // TM_FFN
    return pl.pallas_call(
        _ffn_kernel,
        grid=(m // TM_FFN, nf),
        in_specs=[
            pl.BlockSpec((TM_FFN, d), lambda i, j: (i, 0)),
            pl.BlockSpec((d, TF_FFN), lambda i, j: (0, j)),
            pl.BlockSpec((d, TF_FFN), lambda i, j: (0, j + nf)),
            pl.BlockSpec((TF_FFN, d), lambda i, j: (j, 0)),
            pl.BlockSpec((TM_FFN, d), lambda i, j: (i, 0)),
            pl.BlockSpec((1, N_MOD, d), lambda i, j: (i // tiles_per_seq, 0, 0)),
            pl.BlockSpec((1, d), lambda i, j: (0, 0)),
        ],
        out_specs=pl.BlockSpec((TM_FFN, d), lambda i, j: (i, 0)),
        out_shape=jax.ShapeDtypeStruct((m, d), F32),
        scratch_shapes=[pltpu.VMEM((TM_FFN, d), F32)],
        compiler_params=pltpu.CompilerParams(
            dimension_semantics=("arbitrary", "arbitrary"), vmem_limit_bytes=VMEM_LIMIT),
        name="ffn",
    )(h2, w_in, w_in, w_out, x1, mod, norm_w)


def _split_in_weights(w_in):
    gk0 = MAIN_WIDTH - 2 * D_MODEL
    w_main = jnp.concatenate([w_in[:, gk0 + GK_RANK:], w_in[:, :gk0]], axis=1).astype(BF16)
    w_gk1 = jnp.pad(w_in[:, gk0:gk0 + GK_RANK], ((0, 0), (0, LANES - GK_RANK))).astype(BF16)
    return w_main, w_gk1


def kernel(x, c, w_ada, b_ada, norm1_w, w_in, w_gk2, b_gk2, lb_param, a_norm_w, b_norm_w,
           w_up_a, w_up_b, w_o, norm2_w, w_ffn_in, w_ffn_out, final_norm_w):
    bsz, seq, d = x.shape
    depth = w_in.shape[0]
    assert depth == 1 and d == D_MODEL and w_in.shape[2] == MAIN_WIDTH + GK_RANK
    assert seq % TM_PROJ == 0 and seq % CHUNK == 0 and seq % TM_FFN == 0 and seq % TM_MERGE == 0
    m = bsz * seq
    x2 = x.reshape(m, d)

    mod = _ada(c, w_ada[0], b_ada[0][None, :]).reshape(bsz, N_MOD, d)

    w_main, w_gk1 = _split_in_weights(w_in[0])
    proj, gk_low = _inproj(x2, mod, norm1_w[0][None, :], w_main, w_gk1, seq)

    ya = _hgrn2(proj, lb_param, a_norm_w[0][None, :], bsz, seq)
    w_gk2p = jnp.pad(w_gk2[0], ((0, LANES - GK_RANK), (0, 0))).astype(BF16)
    yb = _gla(proj, gk_low, w_gk2p, b_gk2[0][None, :], b_norm_w[0][None, :], bsz, seq)

    x1, h2 = _merge(ya, yb, proj, x2, mod, norm2_w[0][None, :],
                    w_up_a[0].astype(BF16), w_up_b[0].astype(BF16), w_o[0].astype(BF16), seq)

    out = _ffn(h2, w_ffn_in[0].astype(BF16), w_ffn_out[0].astype(BF16), x1, mod,
               final_norm_w[None, :], seq)
    return out.reshape(bsz, seq, d)
```

```python
import functools
import math

import jax
import jax.numpy as jnp
import numpy as np
from jax import lax
from jax.experimental import pallas as pl
from jax.experimental.pallas import tpu as pltpu

F32 = jnp.float32
BF16 = jnp.bfloat16

D_MODEL = 2048
A_HEADS = 8
A_HEAD_DIM = 128
A_WIDTH = A_HEADS * A_HEAD_DIM
B_HEADS = 4
B_KEY_DIM = 128
B_VAL_DIM = 256
B_KEY_WIDTH = B_HEADS * B_KEY_DIM
B_VAL_WIDTH = B_HEADS * B_VAL_DIM
GK_RANK = 16
GATE_LOGIT_NORMALIZER = 16.0
N_MOD = 6
EPS = 1e-6
LOG2E = math.log2(math.e)

LANES = 128
SUBLANES = 8
VMEM_LIMIT = 56 * 1024 * 1024

MAIN_WIDTH = 4 * A_WIDTH + 2 * B_KEY_WIDTH + 2 * B_VAL_WIDTH + 2 * D_MODEL
COL_MA, COL_MB = 0, D_MODEL
COL_QA = 2 * D_MODEL
COL_FA, COL_IA, COL_GA = COL_QA + A_WIDTH, COL_QA + 2 * A_WIDTH, COL_QA + 3 * A_WIDTH
COL_QB = COL_QA + 4 * A_WIDTH
COL_KB = COL_QB + B_KEY_WIDTH
COL_VB = COL_KB + B_KEY_WIDTH
COL_GB = COL_VB + B_VAL_WIDTH

CHUNK = 128
TM_PROJ = 1024
TN_PROJ = 1024
TM_MERGE = 256
TM_FFN = 512
TF_FFN = 512
TN_ADA = 1024


def _sigmoid(x):
    return 1.0 / (1.0 + jnp.exp2(x * (-LOG2E)))


def _rms_rows(x):
    return x * lax.rsqrt(jnp.mean(x * x, axis=-1, keepdims=True) + EPS)


def _dot(a, b):
    return jnp.dot(a, b, preferred_element_type=F32)


def _dot_nt(a, b):
    return lax.dot_general(a, b, (((1,), (1,)), ((), ())), preferred_element_type=F32)


def _dot_tn(a, b):
    return lax.dot_general(a, b, (((0,), (0,)), ((), ())), preferred_element_type=F32)


def _ada_kernel(c_ref, w_ref, b_ref, o_ref):
    c = c_ref[...]
    cond = (c * _sigmoid(c)).astype(BF16)
    o_ref[...] = _dot(cond, w_ref[...].astype(BF16)) + b_ref[...]


def _ada(c, w, b):
    bsz, d = c.shape
    n = w.shape[1]
    return pl.pallas_call(
        _ada_kernel,
        grid=(n // TN_ADA,),
        in_specs=[
            pl.BlockSpec((bsz, d), lambda j: (0, 0)),
            pl.BlockSpec((d, TN_ADA), lambda j: (0, j)),
            pl.BlockSpec((1, TN_ADA), lambda j: (0, j)),
        ],
        out_specs=pl.BlockSpec((bsz, TN_ADA), lambda j: (0, j)),
        out_shape=jax.ShapeDtypeStruct((bsz, n), F32),
        compiler_params=pltpu.CompilerParams(
            dimension_semantics=("arbitrary",), vmem_limit_bytes=VMEM_LIMIT),
        name="ada",
    )(c, w, b)


def _inproj_kernel(x_ref, mod_ref, nw_ref, w_ref, wgk_ref, o_ref, gk_ref, h_scr):
    @pl.when(pl.program_id(1) == 0)
    def _():
        y = _rms_rows(x_ref[...]) * nw_ref[...]
        h = y * (1.0 + mod_ref[0, 1:2, :]) + mod_ref[0, 0:1, :]
        hb = h.astype(BF16)
        h_scr[...] = hb
        gk_ref[...] = _dot(hb, wgk_ref[...])

    o_ref[...] = _dot(h_scr[...], w_ref[...])


def _inproj(x2, mod, norm_w, w_main, w_gk1, seq):
    m, d = x2.shape
    n = w_main.shape[1]
    tiles_per_seq = seq // TM_PROJ
    return pl.pallas_call(
        _inproj_kernel,
        grid=(m // TM_PROJ, n // TN_PROJ),
        in_specs=[
            pl.BlockSpec((TM_PROJ, d), lambda i, j: (i, 0)),
            pl.BlockSpec((1, N_MOD, d), lambda i, j: (i // tiles_per_seq, 0, 0)),
            pl.BlockSpec((1, d), lambda i, j: (0, 0)),
            pl.BlockSpec((d, TN_PROJ), lambda i, j: (0, j)),
            pl.BlockSpec((d, LANES), lambda i, j: (0, 0)),
        ],
        out_specs=[
            pl.BlockSpec((TM_PROJ, TN_PROJ), lambda i, j: (i, j)),
            pl.BlockSpec((TM_PROJ, LANES), lambda i, j: (i, 0)),
        ],
        out_shape=[
            jax.ShapeDtypeStruct((m, n), F32),
            jax.ShapeDtypeStruct((m, LANES), F32),
        ],
        scratch_shapes=[pltpu.VMEM((TM_PROJ, d), BF16)],
        compiler_params=pltpu.CompilerParams(
            dimension_semantics=("arbitrary", "arbitrary"), vmem_limit_bytes=VMEM_LIMIT),
        name="inproj",
    )(x2, mod, norm_w, w_main, w_gk1)


def _pair_level(c):
    i = lax.broadcasted_iota(jnp.int32, (c, c), 0)
    j = lax.broadcasted_iota(jnp.int32, (c, c), 1)
    x = jnp.where(j < i, i ^ j, 0).astype(F32)
    top = pltpu.bitcast(x, jnp.int32) & jnp.int32(0x7F800000)
    return pltpu.bitcast(top, F32)


def _group_mid_rows(p, g):
    c, dk = p.shape
    p3 = p.reshape(c // SUBLANES, SUBLANES, dk)
    sub = lax.broadcasted_iota(jnp.int32, p3.shape, 1)
    out = None
    for s in range(0, SUBLANES, 2 * g):
        part = jnp.broadcast_to(p3[:, s + g - 1:s + g, :], p3.shape)
        out = part if out is None else jnp.where(sub >= s, part, out)
    return out.reshape(c, dk)


def _chunk_attend(q, k, v, g2, st_ref, pair_lvl, heads):
    c, width = q.shape
    dk = width // heads
    dv = v.shape[1] // heads
    hk = lambda a, h: a[:, h * dk:(h + 1) * dk]
    hv = lambda a, h: a[:, h * dv:(h + 1) * dv]
    row = lax.broadcasted_iota(jnp.int32, (c, width), 0)
    vb = v.astype(BF16)

    p = g2
    att = [jnp.zeros((c, c), F32)] * heads
    g = 1
    while g < c:
        if g < SUBLANES:
            second = (row & g) != 0
            if g == 1:
                t = pltpu.roll(p, 1, axis=0)
                expo = jnp.where(second, p, 0.0)
            else:
                t = _group_mid_rows(p, g)
                expo = jnp.where(second, p, t - p)
            z = jnp.where(second, q, k) * jnp.exp2(expo)
            p = p + jnp.where(second, t, 0.0)
        else:
            zs, ps = [], []
            for s in range(0, c, 2 * g):
                p1, p2 = p[s:s + g, :], p[s + g:s + 2 * g, :]
                t = jnp.broadcast_to(p1[g - 1:g, :], (g, width))
                zs += [k[s:s + g, :] * jnp.exp2(t - p1), q[s + g:s + 2 * g, :] * jnp.exp2(p2)]
                ps += [p1, p2 + t]
            z = jnp.concatenate(zs, axis=0)
            p = jnp.concatenate(ps, axis=0)
        zb = z.astype(BF16)
        mask = pair_lvl == float(g)
        att = [jnp.where(mask, _dot_nt(hk(zb, h), hk(zb, h)), att[h]) for h in range(heads)]
        g *= 2

    qk = q * k
    q_dec = (q * jnp.exp2(p)).astype(BF16)
    p_last = p[c - 1:c, :]
    k_dec = (k * jnp.exp2(p_last - p)).astype(BF16)
    st_scale = jnp.exp2(p_last)
    outs = []
    for h in range(heads):
        st = st_ref[h]
        o = _dot(att[h].astype(BF16), hv(vb, h))
        o = o + jnp.sum(hk(qk, h), axis=-1, keepdims=True) * hv(v, h)
        o = o + _dot_nt(hk(q_dec, h), st.astype(BF16))
        st_ref[h] = st * hk(st_scale, h) + _dot_tn(hv(vb, h), hk(k_dec, h))
        outs.append(o)
    return outs


def _gated_head_norm(o, gate, w):
    return _rms_rows(o) * w * (gate * _sigmoid(gate))


def _hgrn2_kernel(qa_ref, fa_ref, ia_ref, ga_ref, lbp_ref, nw_ref, y_ref, st_ref):
    @pl.when(pl.program_id(1) == 0)
    def _():
        st_ref[...] = jnp.zeros_like(st_ref)

    pair_lvl = _pair_level(CHUNK)
    lbp = lbp_ref[...]
    pe = jnp.exp(lbp - jnp.max(lbp, axis=0, keepdims=True))
    lb_all = pe[0:1, :] / jnp.sum(pe, axis=0, keepdims=True)
    nw = nw_ref[...]
    qa = qa_ref[...]
    f = lb_all + (1.0 - lb_all) * _sigmoid(fa_ref[...])
    outs = _chunk_attend(qa * _sigmoid(qa), 1.0 - f, ia_ref[...], jnp.log2(f), st_ref, pair_lvl, A_HEADS)
    for h, o in enumerate(outs):
        sl = slice(h * A_HEAD_DIM, (h + 1) * A_HEAD_DIM)
        y_ref[:, sl] = _gated_head_norm(o, ga_ref[:, sl], nw).astype(y_ref.dtype)


def _hgrn2(proj, lb_param, norm_w, bsz, seq):
    m = proj.shape[0]
    nc = seq // CHUNK

    def col(c0):
        assert c0 % A_WIDTH == 0
        return pl.BlockSpec((CHUNK, A_WIDTH), lambda b, c: (b * nc + c, c0 // A_WIDTH))

    return pl.pallas_call(
        _hgrn2_kernel,
        grid=(bsz, nc),
        in_specs=[
            col(COL_QA), col(COL_FA), col(COL_IA), col(COL_GA),
            pl.BlockSpec(lb_param.shape, lambda b, c: (0, 0)),
            pl.BlockSpec((1, A_HEAD_DIM), lambda b, c: (0, 0)),
        ],
        out_specs=pl.BlockSpec((CHUNK, A_WIDTH), lambda b, c: (b * nc + c, 0)),
        out_shape=jax.ShapeDtypeStruct((m, A_WIDTH), BF16),
        scratch_shapes=[pltpu.VMEM((A_HEADS, A_HEAD_DIM, A_HEAD_DIM), F32)],
        compiler_params=pltpu.CompilerParams(
            dimension_semantics=("arbitrary", "arbitrary"), vmem_limit_bytes=VMEM_LIMIT),
        name="hgrn2",
    )(proj, proj, proj, proj, lb_param, norm_w)


def _gla_kernel(qb_ref, kb_ref, vb_ref, gb_ref, gkl_ref, wgk_ref, bgk_ref, nw_ref, y_ref, st_ref):
    @pl.when(pl.program_id(1) == 0)
    def _():
        st_ref[...] = jnp.zeros_like(st_ref)

    pair_lvl = _pair_level(CHUNK)
    z = _dot(gkl_ref[...].astype(BF16), wgk_ref[...]) + bgk_ref[...]
    z2 = z * LOG2E
    gk = (jnp.minimum(z2, 0.0) - jnp.log2(1.0 + jnp.exp2(-jnp.abs(z2)))) * (1.0 / GATE_LOGIT_NORMALIZER)
    nw = nw_ref[...]
    scale = B_KEY_DIM ** -0.5
    outs = _chunk_attend(qb_ref[...] * scale, kb_ref[...], vb_ref[...], gk, st_ref, pair_lvl, B_HEADS)
    for h, o in enumerate(outs):
        vs = slice(h * B_VAL_DIM, (h + 1) * B_VAL_DIM)
        y_ref[:, vs] = _gated_head_norm(o, gb_ref[:, vs], nw).astype(y_ref.dtype)


def _gla(proj, gk_low, w_gk2p, b_gk2, norm_w, bsz, seq):
    m = proj.shape[0]
    nc = seq // CHUNK

    def col(c0, width):
        assert c0 % width == 0
        return pl.BlockSpec((CHUNK, width), lambda b, c: (b * nc + c, c0 // width))

    return pl.pallas_call(
        _gla_kernel,
        grid=(bsz, nc),
        in_specs=[
            col(COL_QB, B_KEY_WIDTH), col(COL_KB, B_KEY_WIDTH),
            col(COL_VB, B_VAL_WIDTH), col(COL_GB, B_VAL_WIDTH),
            pl.BlockSpec((CHUNK, LANES), lambda b, c: (b * nc + c, 0)),
            pl.BlockSpec((LANES, B_KEY_WIDTH), lambda b, c: (0, 0)),
            pl.BlockSpec((1, B_KEY_WIDTH), lambda b, c: (0, 0)),
            pl.BlockSpec((1, B_VAL_DIM), lambda b, c: (0, 0)),
        ],
        out_specs=pl.BlockSpec((CHUNK, B_VAL_WIDTH), lambda b, c: (b * nc + c, 0)),
        out_shape=jax.ShapeDtypeStruct((m, B_VAL_WIDTH), BF16),
        scratch_shapes=[pltpu.VMEM((B_HEADS, B_VAL_DIM, B_KEY_DIM), F32)],
        compiler_params=pltpu.CompilerParams(
            dimension_semantics=("arbitrary", "arbitrary"), vmem_limit_bytes=VMEM_LIMIT),
        name="gla",
    )(proj, proj, proj, proj, gk_low, w_gk2p, b_gk2, norm_w)


def _merge_kernel(ya_ref, yb_ref, ma_ref, mb_ref, x_ref, mod_ref, nw_ref,
                  wua_ref, wub_ref, wo_ref, x1_ref, h2_ref):
    ua = _dot(ya_ref[...], wua_ref[...])
    ub = _dot(yb_ref[...], wub_ref[...])
    merged = _sigmoid(ma_ref[...]) * ua + _sigmoid(mb_ref[...]) * ub
    out = _dot(merged.astype(BF16), wo_ref[...])
    x1 = x_ref[...] + mod_ref[0, 2:3, :] * out
    x1_ref[...] = x1
    y = _rms_rows(x1) * nw_ref[...]
    h2_ref[...] = (y * (1.0 + mod_ref[0, 4:5, :]) + mod_ref[0, 3:4, :]).astype(h2_ref.dtype)


def _merge(ya, yb, proj, x2, mod, norm_w, w_up_a, w_up_b, w_o, seq):
    m, d = x2.shape
    assert COL_MA % d == 0 and COL_MB % d == 0
    tiles_per_seq = seq // TM_MERGE
    row = lambda i: (i, 0)
    const = lambda i: (0, 0)
    return pl.pallas_call(
        _merge_kernel,
        grid=(m // TM_MERGE,),
        in_specs=[
            pl.BlockSpec((TM_MERGE, A_WIDTH), row),
            pl.BlockSpec((TM_MERGE, B_VAL_WIDTH), row),
            pl.BlockSpec((TM_MERGE, d), lambda i: (i, COL_MA // d)),
            pl.BlockSpec((TM_MERGE, d), lambda i: (i, COL_MB // d)),
            pl.BlockSpec((TM_MERGE, d), row),
            pl.BlockSpec((1, N_MOD, d), lambda i: (i // tiles_per_seq, 0, 0)),
            pl.BlockSpec((1, d), const),
            pl.BlockSpec(w_up_a.shape, const),
            pl.BlockSpec(w_up_b.shape, const),
            pl.BlockSpec(w_o.shape, const),
        ],
        out_specs=[pl.BlockSpec((TM_MERGE, d), row), pl.BlockSpec((TM_MERGE, d), row)],
        out_shape=[jax.ShapeDtypeStruct((m, d), F32), jax.ShapeDtypeStruct((m, d), BF16)],
        compiler_params=pltpu.CompilerParams(
            dimension_semantics=("arbitrary",), vmem_limit_bytes=VMEM_LIMIT),
        name="merge",
    )(ya, yb, proj, proj, x2, mod, norm_w, w_up_a, w_up_b, w_o)


def _ffn_kernel(h_ref, wg_ref, wu_ref, wo_ref, x1_ref, mod_ref, nw_ref, o_ref, acc_ref):
    j = pl.program_id(1)

    @pl.when(j == 0)
    def _():
        acc_ref[...] = jnp.zeros_like(acc_ref)

    h = h_ref[...]
    gate = _dot(h, wg_ref[...])
    up = _dot(h, wu_ref[...])
    act = (gate * _sigmoid(gate) * up).astype(BF16)
    acc_ref[...] += _dot(act, wo_ref[...])

    @pl.when(j == pl.num_programs(1) - 1)
    def _():
        x2 = x1_ref[...] + mod_ref[0, 5:6, :] * acc_ref[...]
        o_ref[...] = _rms_rows(x2) * nw_ref[...]


def _ffn(h2, w_in, w_out, x1, mod, norm_w, seq):
    m, d = x1.shape
    hidden = w_out.shape[0]
    nf = hidden // TF_FFN
    tiles_per_seq = seq // TM_FFN
    return pl.pallas_call(
        _ffn_kernel,
        grid=(m // TM_FFN, nf),
        in_specs=[
            pl.BlockSpec((TM_FFN, d), lambda i, j: (i, 0)),
            pl.BlockSpec((d, TF_FFN), lambda i, j: (0, j)),
            pl.BlockSpec((d, TF_FFN), lambda i, j: (0, j + nf)),
            pl.BlockSpec((TF_FFN, d), lambda i, j: (j, 0)),
            pl.BlockSpec((TM_FFN, d), lambda i, j: (i, 0)),
            pl.BlockSpec((1, N_MOD, d), lambda i, j: (i // tiles_per_seq, 0, 0)),
            pl.BlockSpec((1, d), lambda i, j: (0, 0)),
        ],
        out_specs=pl.BlockSpec((TM_FFN, d), lambda i, j: (i, 0)),
        out_shape=jax.ShapeDtypeStruct((m, d), F32),
        scratch_shapes=[pltpu.VMEM((TM_FFN, d), F32)],
        compiler_params=pltpu.CompilerParams(
            dimension_semantics=("arbitrary", "arbitrary"), vmem_limit_bytes=VMEM_LIMIT),
        name="ffn",
    )(h2, w_in, w_in, w_out, x1, mod, norm_w)


def _split_in_weights(w_in):
    gk0 = MAIN_WIDTH - 2 * D_MODEL
    w_main = jnp.concatenate([w_in[:, gk0 + GK_RANK:].astype(BF16), w_in[:, :gk0].astype(BF16)], axis=1)
    w_gk1 = jnp.pad(w_in[:, gk0:gk0 + GK_RANK].astype(BF16), ((0, 0), (0, LANES - GK_RANK)))
    return w_main, w_gk1


def kernel(x, c, w_ada, b_ada, norm1_w, w_in, w_gk2, b_gk2, lb_param, a_norm_w, b_norm_w,
           w_up_a, w_up_b, w_o, norm2_w, w_ffn_in, w_ffn_out, final_norm_w):
    bsz, seq, d = x.shape
    depth = w_in.shape[0]
    assert depth == 1 and d == D_MODEL and w_in.shape[2] == MAIN_WIDTH + GK_RANK
    assert seq % TM_PROJ == 0 and seq % CHUNK == 0 and seq % TM_FFN == 0 and seq % TM_MERGE == 0
    m = bsz * seq
    x2 = x.reshape(m, d)

    def layer0(w):
        return w.reshape(w.shape[1:])

    mod = _ada(c, layer0(w_ada), b_ada).reshape(bsz, N_MOD, d)

    w_main, w_gk1 = _split_in_weights(layer0(w_in))
    proj, gk_low = _inproj(x2, mod, norm1_w, w_main, w_gk1, seq)

    ya = _hgrn2(proj, lb_param, a_norm_w, bsz, seq)
    w_gk2p = jnp.pad(layer0(w_gk2).astype(BF16), ((0, LANES - GK_RANK), (0, 0)))
    yb = _gla(proj, gk_low, w_gk2p, b_gk2, b_norm_w, bsz, seq)

    x1, h2 = _merge(ya, yb, proj, x2, mod, norm2_w, layer0(w_up_a).astype(BF16),
                    layer0(w_up_b).astype(BF16), layer0(w_o).astype(BF16), seq)

    out = _ffn(h2, layer0(w_ffn_in).astype(BF16), layer0(w_ffn_out).astype(BF16), x1, mod,
               final_norm_w.reshape(1, d), seq)
    return out.reshape(bsz, seq, d)
```

```python
import functools
import math

import jax
import jax.numpy as jnp
import numpy as np
from jax import lax
from jax.experimental import pallas as pl
from jax.experimental.pallas import tpu as pltpu

F32 = jnp.float32
BF16 = jnp.bfloat16

D_MODEL = 2048
A_HEADS = 8
A_HEAD_DIM = 128
A_WIDTH = A_HEADS * A_HEAD_DIM
B_HEADS = 4
B_KEY_DIM = 128
B_VAL_DIM = 256
B_KEY_WIDTH = B_HEADS * B_KEY_DIM
B_VAL_WIDTH = B_HEADS * B_VAL_DIM
GK_RANK = 16
GATE_LOGIT_NORMALIZER = 16.0
N_MOD = 6
EPS = 1e-6
LOG2E = math.log2(math.e)

LANES = 128
SUBLANES = 8
VMEM_LIMIT = 56 * 1024 * 1024

MAIN_WIDTH = 4 * A_WIDTH + 2 * B_KEY_WIDTH + 2 * B_VAL_WIDTH + 2 * D_MODEL
COL_MA, COL_MB = 0, D_MODEL
COL_QA = 2 * D_MODEL
COL_FA, COL_IA, COL_GA = COL_QA + A_WIDTH, COL_QA + 2 * A_WIDTH, COL_QA + 3 * A_WIDTH
COL_QB = COL_QA + 4 * A_WIDTH
COL_KB = COL_QB + B_KEY_WIDTH
COL_VB = COL_KB + B_KEY_WIDTH
COL_GB = COL_VB + B_VAL_WIDTH

CHUNK = 128
TM_PROJ = 1024
TN_PROJ = 1024
TM_MERGE = 256
TM_FFN_UP = 2048
TF_FFN = 512
TM_FFN_DOWN = 512
TN_ADA = 1024


def _sigmoid(x):
    return 1.0 / (1.0 + jnp.exp2(x * (-LOG2E)))


def _rms_rows(x):
    return x * lax.rsqrt(jnp.mean(x * x, axis=-1, keepdims=True) + EPS)


def _dot(a, b):
    return jnp.dot(a, b, preferred_element_type=F32)


def _dot_nt(a, b):
    return lax.dot_general(a, b, (((1,), (1,)), ((), ())), preferred_element_type=F32)


def _dot_tn(a, b):
    return lax.dot_general(a, b, (((0,), (0,)), ((), ())), preferred_element_type=F32)


def _ada_kernel(c_ref, w_ref, b_ref, o_ref):
    c = c_ref[...]
    cond = (c * _sigmoid(c)).astype(BF16)
    o_ref[...] = _dot(cond, w_ref[...].astype(BF16)) + b_ref[...]


def _ada(c, w, b):
    bsz, d = c.shape
    n = w.shape[1]
    return pl.pallas_call(
        _ada_kernel,
        grid=(n // TN_ADA,),
        in_specs=[
            pl.BlockSpec((bsz, d), lambda j: (0, 0)),
            pl.BlockSpec((d, TN_ADA), lambda j: (0, j)),
            pl.BlockSpec((1, TN_ADA), lambda j: (0, j)),
        ],
        out_specs=pl.BlockSpec((bsz, TN_ADA), lambda j: (0, j)),
        out_shape=jax.ShapeDtypeStruct((bsz, n), F32),
        compiler_params=pltpu.CompilerParams(
            dimension_semantics=("arbitrary",), vmem_limit_bytes=VMEM_LIMIT),
        name="ada",
    )(c, w, b)


def _inproj_kernel(n_gate_tiles, x_ref, mod_ref, nw_ref, wg_ref, wm_ref, wgk_ref, o_ref, gk_ref, h_scr):
    j = pl.program_id(1)

    @pl.when(j == 0)
    def _():
        y = _rms_rows(x_ref[...]) * nw_ref[...]
        h = y * (1.0 + mod_ref[0, 1:2, :]) + mod_ref[0, 0:1, :]
        hb = h.astype(BF16)
        h_scr[...] = hb
        gk_ref[...] = _dot(hb, wgk_ref[...])

    @pl.when(j < n_gate_tiles)
    def _():
        o_ref[...] = _dot(h_scr[...], wg_ref[...])

    @pl.when(j >= n_gate_tiles)
    def _():
        o_ref[...] = _dot(h_scr[...], wm_ref[...])


def _inproj(x2, mod, norm_w, w_gates, w_mix, w_gk1, seq):
    m, d = x2.shape
    assert w_gates.shape[1] % TN_PROJ == 0 and w_mix.shape[1] % TN_PROJ == 0
    ng = w_gates.shape[1] // TN_PROJ
    n = w_gates.shape[1] + w_mix.shape[1]
    tiles_per_seq = seq // TM_PROJ
    return pl.pallas_call(
        functools.partial(_inproj_kernel, ng),
        grid=(m // TM_PROJ, n // TN_PROJ),
        in_specs=[
            pl.BlockSpec((TM_PROJ, d), lambda i, j: (i, 0)),
            pl.BlockSpec((1, N_MOD, d), lambda i, j: (i // tiles_per_seq, 0, 0)),
            pl.BlockSpec((1, d), lambda i, j: (0, 0)),
            pl.BlockSpec((d, TN_PROJ), lambda i, j: (0, jnp.minimum(j, ng - 1))),
            pl.BlockSpec((d, TN_PROJ), lambda i, j: (0, jnp.maximum(j - ng, 0))),
            pl.BlockSpec((d, LANES), lambda i, j: (0, 0)),
        ],
        out_specs=[
            pl.BlockSpec((TM_PROJ, TN_PROJ), lambda i, j: (i, j)),
            pl.BlockSpec((TM_PROJ, LANES), lambda i, j: (i, 0)),
        ],
        out_shape=[
            jax.ShapeDtypeStruct((m, n), F32),
            jax.ShapeDtypeStruct((m, LANES), F32),
        ],
        scratch_shapes=[pltpu.VMEM((TM_PROJ, d), BF16)],
        compiler_params=pltpu.CompilerParams(
            dimension_semantics=("arbitrary", "arbitrary"), vmem_limit_bytes=VMEM_LIMIT),
        name="inproj",
    )(x2, mod, norm_w, w_gates, w_mix, w_gk1)


class _ScoreMasks:
    def __init__(self, c):
        sub = lax.broadcasted_iota(jnp.int32, (SUBLANES, c), 0)
        self.lane = lax.broadcasted_iota(jnp.int32, (SUBLANES, c), 1)
        in_tile = self.lane & (SUBLANES - 1)
        self.low_bit = sub ^ in_tile
        self.diag = jnp.where(in_tile < sub, self.lane >> int(math.log2(SUBLANES)), -1)

    def level(self, g):
        return (self.low_bit & -g) == g


def _group_mid_rows(p, g):
    c, dk = p.shape
    p3 = p.reshape(c // SUBLANES, SUBLANES, dk)
    sub = lax.broadcasted_iota(jnp.int32, p3.shape, 1)
    out = None
    for s in range(0, SUBLANES, 2 * g):
        part = jnp.broadcast_to(p3[:, s + g - 1:s + g, :], p3.shape)
        out = part if out is None else jnp.where(sub >= s, part, out)
    return out.reshape(c, dk)


def _chunk_attend(q, k, g2, v_heads, st_refs):
    c, width = q.shape
    heads = len(v_heads)
    dk = width // heads
    groups = c // SUBLANES
    hk = lambda a, h: a[:, h * dk:(h + 1) * dk]
    rows = lambda a, r: a[r * SUBLANES:(r + 1) * SUBLANES, :]
    row = lax.broadcasted_iota(jnp.int32, (c, width), 0)
    masks = _ScoreMasks(c)

    p = g2
    small = [[None] * groups for _ in range(heads)]
    big = [[None] * groups for _ in range(heads)]
    filled = [0] * groups
    g = c // 2
    plan = []
    while g >= SUBLANES:
        plan.append(g)
        g //= 2

    g = 1
    big_scores = {}
    while g < c:
        if g < SUBLANES:
            second = (row & g) != 0
            if g == 1:
                t = pltpu.roll(p, 1, axis=0)
                expo = jnp.where(second, p, 0.0)
            else:
                t = _group_mid_rows(p, g)
                expo = jnp.where(second, p, t - p)
            z = jnp.where(second, q, k) * jnp.exp2(expo)
            p = p + jnp.where(second, t, 0.0)
            zb = z.astype(BF16)
            lvl = None if g == 1 else masks.level(g)
            for h in range(heads):
                s = _dot_nt(hk(zb, h), hk(zb, h))
                for r in range(groups):
                    small[h][r] = rows(s, r) if g == 1 else jnp.where(lvl, rows(s, r), small[h][r])
        else:
            zs, ps = [], []
            for s0 in range(0, c, 2 * g):
                p1, p2 = p[s0:s0 + g, :], p[s0 + g:s0 + 2 * g, :]
                t = jnp.broadcast_to(p1[g - 1:g, :], (g, width))
                zs += [k[s0:s0 + g, :] * jnp.exp2(t - p1), q[s0 + g:s0 + 2 * g, :] * jnp.exp2(p2)]
                ps += [p1, p2 + t]
            zb = jnp.concatenate(zs, axis=0).astype(BF16)
            p = jnp.concatenate(ps, axis=0)
            zq = jnp.concatenate([zb[s0 + g:s0 + 2 * g, :] for s0 in range(0, c, 2 * g)], axis=0)
            big_scores[g] = [_dot_nt(hk(zq, h), hk(zb, h)) for h in range(heads)]
        g *= 2

    att = []
    for h in range(heads):
        out_rows = []
        for r in range(groups):
            first_row = r * SUBLANES
            acc, edge = None, 0
            for g in plan:
                if first_row & g:
                    blk = (first_row // (2 * g)) * g + (first_row % g)
                    piece = big_scores[g][h][blk:blk + SUBLANES, :]
                    acc = piece if acc is None else jnp.where(masks.lane < edge, acc, piece)
                    edge += g
            if acc is None:
                acc = jnp.zeros((SUBLANES, c), F32)
            else:
                acc = jnp.where(masks.lane < edge, acc, 0.0)
            out_rows.append(jnp.where(masks.diag == r, small[h][r], acc))
        att.append(jnp.concatenate(out_rows, axis=0).astype(BF16))

    qk = q * k
    q_dec = (q * jnp.exp2(p)).astype(BF16)
    p_last = p[c - 1:c, :]
    k_dec = (k * jnp.exp2(p_last - p)).astype(BF16)
    st_scale = jnp.exp2(p_last)
    outs = []
    for h in range(heads):
        v = v_heads[h]
        vb = v.astype(BF16)
        st = st_refs[h][...]
        o = _dot(att[h], vb)
        o = o + jnp.sum(hk(qk, h), axis=-1, keepdims=True) * v
        o = o + _dot_nt(hk(q_dec, h), st.astype(BF16))
        st_refs[h][...] = st * hk(st_scale, h) + _dot_tn(vb, hk(k_dec, h))
        outs.append(o)
    return outs


def _gated_head_norm(o, gate, w):
    return _rms_rows(o) * w * (gate * _sigmoid(gate))


def _mixers_kernel(qa_ref, fa_ref, ia_ref, ga_ref, qb_ref, kb_ref, vb_ref, gb_ref, gkl_ref,
                   lbp_ref, wgk_ref, bgk_ref, anw_ref, bnw_ref, ya_ref, yb_ref, sta_ref, stb_ref):
    @pl.when(pl.program_id(1) == 0)
    def _():
        sta_ref[...] = jnp.zeros_like(sta_ref)
        stb_ref[...] = jnp.zeros_like(stb_ref)

    lbp = lbp_ref[...]
    pe = jnp.exp(lbp - jnp.max(lbp, axis=0, keepdims=True))
    lb = pe[0:1, :] / jnp.sum(pe, axis=0, keepdims=True)
    qa = qa_ref[...]
    f = lb + (1.0 - lb) * _sigmoid(fa_ref[...])
    z2 = (_dot(gkl_ref[...].astype(BF16), wgk_ref[...]) + bgk_ref[...]) * LOG2E
    gk = (jnp.minimum(z2, 0.0) - jnp.log2(1.0 + jnp.exp2(-jnp.abs(z2)))) * (1.0 / GATE_LOGIT_NORMALIZER)

    q = jnp.concatenate([qa * _sigmoid(qa), qb_ref[...] * (B_KEY_DIM ** -0.5)], axis=1)
    k = jnp.concatenate([1.0 - f, kb_ref[...]], axis=1)
    g2 = jnp.concatenate([jnp.log2(f), gk], axis=1)
    a_cols = [slice(h * A_HEAD_DIM, (h + 1) * A_HEAD_DIM) for h in range(A_HEADS)]
    b_cols = [slice(h * B_VAL_DIM, (h + 1) * B_VAL_DIM) for h in range(B_HEADS)]
    v_heads = [ia_ref[:, sl] for sl in a_cols] + [vb_ref[:, sl] for sl in b_cols]
    st_refs = [sta_ref.at[h] for h in range(A_HEADS)] + [stb_ref.at[h] for h in range(B_HEADS)]
    outs = _chunk_attend(q, k, g2, v_heads, st_refs)

    anw, bnw = anw_ref[...], bnw_ref[...]
    for sl, o in zip(a_cols, outs[:A_HEADS]):
        ya_ref[:, sl] = _gated_head_norm(o, ga_ref[:, sl], anw).astype(ya_ref.dtype)
    for sl, o in zip(b_cols, outs[A_HEADS:]):
        yb_ref[:, sl] = _gated_head_norm(o, gb_ref[:, sl], bnw).astype(yb_ref.dtype)


def _mixers(proj, gk_low, lb_param, w_gk2p, b_gk2, a_norm_w, b_norm_w, bsz, seq):
    m = proj.shape[0]
    nc = seq // CHUNK
    assert A_HEAD_DIM == B_KEY_DIM

    def col(c0, width):
        assert c0 % width == 0
        return pl.BlockSpec((CHUNK, width), lambda b, c: (b * nc + c, c0 // width))

    const = lambda b, c: (0, 0)
    return pl.pallas_call(
        _mixers_kernel,
        grid=(bsz, nc),
        in_specs=[
            col(COL_QA, A_WIDTH), col(COL_FA, A_WIDTH), col(COL_IA, A_WIDTH), col(COL_GA, A_WIDTH),
            col(COL_QB, B_KEY_WIDTH), col(COL_KB, B_KEY_WIDTH),
            col(COL_VB, B_VAL_WIDTH), col(COL_GB, B_VAL_WIDTH),
            pl.BlockSpec((CHUNK, LANES), lambda b, c: (b * nc + c, 0)),
            pl.BlockSpec(lb_param.shape, const),
            pl.BlockSpec((LANES, B_KEY_WIDTH), const),
            pl.BlockSpec((1, B_KEY_WIDTH), const),
            pl.BlockSpec((1, A_HEAD_DIM), const),
            pl.BlockSpec((1, B_VAL_DIM), const),
        ],
        out_specs=[
            pl.BlockSpec((CHUNK, A_WIDTH), lambda b, c: (b * nc + c, 0)),
            pl.BlockSpec((CHUNK, B_VAL_WIDTH), lambda b, c: (b * nc + c, 0)),
        ],
        out_shape=[
            jax.ShapeDtypeStruct((m, A_WIDTH), BF16),
            jax.ShapeDtypeStruct((m, B_VAL_WIDTH), BF16),
        ],
        scratch_shapes=[
            pltpu.VMEM((A_HEADS, A_HEAD_DIM, A_HEAD_DIM), F32),
            pltpu.VMEM((B_HEADS, B_VAL_DIM, B_KEY_DIM), F32),
        ],
        compiler_params=pltpu.CompilerParams(
            dimension_semantics=("arbitrary", "arbitrary"), vmem_limit_bytes=VMEM_LIMIT),
        name="mixers",
    )(proj, proj, proj, proj, proj, proj, proj, proj, gk_low, lb_param, w_gk2p, b_gk2, a_norm_w, b_norm_w)


def _merge_kernel(ya_ref, yb_ref, ma_ref, mb_ref, x_ref, mod_ref, nw_ref,
                  wua_ref, wub_ref, wo_ref, x1_ref, h2_ref):
    ua = _dot(ya_ref[...], wua_ref[...])
    ub = _dot(yb_ref[...], wub_ref[...])
    merged = _sigmoid(ma_ref[...]) * ua + _sigmoid(mb_ref[...]) * ub
    out = _dot(merged.astype(BF16), wo_ref[...])
    x1 = x_ref[...] + mod_ref[0, 2:3, :] * out
    x1_ref[...] = x1
    y = _rms_rows(x1) * nw_ref[...]
    h2_ref[...] = (y * (1.0 + mod_ref[0, 4:5, :]) + mod_ref[0, 3:4, :]).astype(h2_ref.dtype)


def _merge(ya, yb, proj, x2, mod, norm_w, w_up_a, w_up_b, w_o, seq):
    m, d = x2.shape
    assert COL_MA % d == 0 and COL_MB % d == 0
    tiles_per_seq = seq // TM_MERGE
    row = lambda i: (i, 0)
    const = lambda i: (0, 0)
    return pl.pallas_call(
        _merge_kernel,
        grid=(m // TM_MERGE,),
        in_specs=[
            pl.BlockSpec((TM_MERGE, A_WIDTH), row),
            pl.BlockSpec((TM_MERGE, B_VAL_WIDTH), row),
            pl.BlockSpec((TM_MERGE, d), lambda i: (i, COL_MA // d)),
            pl.BlockSpec((TM_MERGE, d), lambda i: (i, COL_MB // d)),
            pl.BlockSpec((TM_MERGE, d), row),
            pl.BlockSpec((1, N_MOD, d), lambda i: (i // tiles_per_seq, 0, 0)),
            pl.BlockSpec((1, d), const),
            pl.BlockSpec(w_up_a.shape, const),
            pl.BlockSpec(w_up_b.shape, const),
            pl.BlockSpec(w_o.shape, const),
        ],
        out_specs=[pl.BlockSpec((TM_MERGE, d), row), pl.BlockSpec((TM_MERGE, d), row)],
        out_shape=[jax.ShapeDtypeStruct((m, d), F32), jax.ShapeDtypeStruct((m, d), BF16)],
        compiler_params=pltpu.CompilerParams(
            dimension_semantics=("arbitrary",), vmem_limit_bytes=VMEM_LIMIT),
        name="merge",
    )(ya, yb, proj, proj, x2, mod, norm_w, w_up_a, w_up_b, w_o)


def _ffn_up_kernel(h_ref, wg_ref, wu_ref, o_ref):
    h = h_ref[...]
    gate = _dot(h, wg_ref[...])
    up = _dot(h, wu_ref[...])
    o_ref[...] = (gate * _sigmoid(gate) * up).astype(o_ref.dtype)


def _ffn_up(h2, w_in):
    m, d = h2.shape
    hidden = w_in.shape[1] // 2
    nf = hidden // TF_FFN
    return pl.pallas_call(
        _ffn_up_kernel,
        grid=(m // TM_FFN_UP, nf),
        in_specs=[
            pl.BlockSpec((TM_FFN_UP, d), lambda i, j: (i, 0)),
            pl.BlockSpec((d, TF_FFN), lambda i, j: (0, j)),
            pl.BlockSpec((d, TF_FFN), lambda i, j: (0, j + nf)),
        ],
        out_specs=pl.BlockSpec((TM_FFN_UP, TF_FFN), lambda i, j: (i, j)),
        out_shape=jax.ShapeDtypeStruct((m, hidden), BF16),
        compiler_params=pltpu.CompilerParams(
            dimension_semantics=("arbitrary", "arbitrary"), vmem_limit_bytes=VMEM_LIMIT),
        name="ffn_up",
    )(h2, w_in, w_in)


def _ffn_down_kernel(a_ref, wo_ref, x1_ref, mod_ref, nw_ref, o_ref):
    x2 = x1_ref[...] + mod_ref[0, 5:6, :] * _dot(a_ref[...], wo_ref[...])
    o_ref[...] = _rms_rows(x2) * nw_ref[...]


def _ffn_down(act, w_out, x1, mod, norm_w, seq):
    m, d = x1.shape
    hidden = w_out.shape[0]
    tiles_per_seq = seq // TM_FFN_DOWN
    return pl.pallas_call(
        _ffn_down_kernel,
        grid=(m // TM_FFN_DOWN,),
        in_specs=[
            pl.BlockSpec((TM_FFN_DOWN, hidden), lambda i: (i, 0)),
            pl.BlockSpec((hidden, d), lambda i: (0, 0)),
            pl.BlockSpec((TM_FFN_DOWN, d), lambda i: (i, 0)),
            pl.BlockSpec((1, N_MOD, d), lambda i: (i // tiles_per_seq, 0, 0)),
            pl.BlockSpec((1, d), lambda i: (0, 0)),
        ],
        out_specs=pl.BlockSpec((TM_FFN_DOWN, d), lambda i: (i, 0)),
        out_shape=jax.ShapeDtypeStruct((m, d), F32),
        compiler_params=pltpu.CompilerParams(
            dimension_semantics=("arbitrary",), vmem_limit_bytes=VMEM_LIMIT),
        name="ffn_down",
    )(act, w_out, x1, mod, norm_w)


def _split_in_weights(w_in):
    gk0 = MAIN_WIDTH - 2 * D_MODEL
    w_gates = w_in[:, gk0 + GK_RANK:].astype(BF16)
    w_mix = w_in[:, :gk0].astype(BF16)
    w_gk1 = jnp.pad(w_in[:, gk0:gk0 + GK_RANK].astype(BF16), ((0, 0), (0, LANES - GK_RANK)))
    return w_gates, w_mix, w_gk1


def kernel(x, c, w_ada, b_ada, norm1_w, w_in, w_gk2, b_gk2, lb_param, a_norm_w, b_norm_w,
           w_up_a, w_up_b, w_o, norm2_w, w_ffn_in, w_ffn_out, final_norm_w):
    bsz, seq, d = x.shape
    depth = w_in.shape[0]
    assert depth == 1 and d == D_MODEL and w_in.shape[2] == MAIN_WIDTH + GK_RANK
    assert seq % TM_PROJ == 0 and seq % CHUNK == 0 and seq % TM_MERGE == 0
    assert seq % TM_FFN_UP == 0 and seq % TM_FFN_DOWN == 0
    m = bsz * seq
    x2 = x.reshape(m, d)

    def layer0(w):
        return w.reshape(w.shape[1:])

    mod = _ada(c, layer0(w_ada), b_ada).reshape(bsz, N_MOD, d)

    w_gates, w_mix, w_gk1 = _split_in_weights(layer0(w_in))
    proj, gk_low = _inproj(x2, mod, norm1_w, w_gates, w_mix, w_gk1, seq)

    w_gk2p = jnp.pad(layer0(w_gk2).astype(BF16), ((0, LANES - GK_RANK), (0, 0)))
    ya, yb = _mixers(proj, gk_low, lb_param, w_gk2p, b_gk2, a_norm_w, b_norm_w, bsz, seq)

    x1, h2 = _merge(ya, yb, proj, x2, mod, norm2_w, layer0(w_up_a).astype(BF16),
                    layer0(w_up_b).astype(BF16), layer0(w_o).astype(BF16), seq)

    act = _ffn_up(h2, layer0(w_ffn_in).astype(BF16))
    out = _ffn_down(act, layer0(w_ffn_out).astype(BF16), x1, mod, final_norm_w.reshape(1, d), seq)
    return out.reshape(bsz, seq, d)
```

```python
import functools
import math

import jax
import jax.numpy as jnp
from jax import lax
from jax.experimental import pallas as pl
from jax.experimental.pallas import tpu as pltpu

F32 = jnp.float32
BF16 = jnp.bfloat16

D_MODEL = 2048
A_HEADS = 8
A_HEAD_DIM = 128
A_WIDTH = A_HEADS * A_HEAD_DIM
B_HEADS = 4
B_KEY_DIM = 128
B_VAL_DIM = 256
B_KEY_WIDTH = B_HEADS * B_KEY_DIM
B_VAL_WIDTH = B_HEADS * B_VAL_DIM
GK_RANK = 16
GATE_LOGIT_NORMALIZER = 16.0
N_MOD = 6
EPS = 1e-6
LOG2E = math.log2(math.e)

LANES = 128
SUBLANES = 8
VMEM_LIMIT = 56 * 1024 * 1024

MAIN_WIDTH = 4 * A_WIDTH + 2 * B_KEY_WIDTH + 2 * B_VAL_WIDTH + 2 * D_MODEL
COL_MA, COL_MB = 0, D_MODEL
COL_QA = 2 * D_MODEL
COL_FA, COL_IA, COL_GA = COL_QA + A_WIDTH, COL_QA + 2 * A_WIDTH, COL_QA + 3 * A_WIDTH
COL_QB = COL_QA + 4 * A_WIDTH
COL_KB = COL_QB + B_KEY_WIDTH
COL_VB = COL_KB + B_KEY_WIDTH
COL_GB = COL_VB + B_VAL_WIDTH

CHUNK = 128
CHUNKS_PER_STEP = 2
TM_PROJ = 1024
TN_PROJ = 1024
TM_MERGE = 256
TM_FFN_UP = 2048
TF_FFN = 512
TM_FFN_DOWN = 512
TN_ADA = 1024


def _sigmoid(x):
    return 1.0 / (1.0 + jnp.exp2(x * (-LOG2E)))


def _rms_rows(x):
    return x * lax.rsqrt(jnp.mean(x * x, axis=-1, keepdims=True) + EPS)


def _dot(a, b):
    return jnp.dot(a, b, preferred_element_type=F32)


def _dot_nt(a, b):
    return lax.dot_general(a, b, (((1,), (1,)), ((), ())), preferred_element_type=F32)


def _dot_tn(a, b):
    return lax.dot_general(a, b, (((0,), (0,)), ((), ())), preferred_element_type=F32)


def _ada_kernel(c_ref, w_ref, b_ref, o_ref):
    c = c_ref[...]
    cond = (c * _sigmoid(c)).astype(BF16)
    o_ref[...] = _dot(cond, w_ref[...].astype(BF16)) + b_ref[...]


def _ada(c, w, b):
    bsz, d = c.shape
    n = w.shape[1]
    return pl.pallas_call(
        _ada_kernel,
        grid=(n // TN_ADA,),
        in_specs=[
            pl.BlockSpec((bsz, d), lambda j: (0, 0)),
            pl.BlockSpec((d, TN_ADA), lambda j: (0, j)),
            pl.BlockSpec((1, TN_ADA), lambda j: (0, j)),
        ],
        out_specs=pl.BlockSpec((bsz, TN_ADA), lambda j: (0, j)),
        out_shape=jax.ShapeDtypeStruct((bsz, n), F32),
        compiler_params=pltpu.CompilerParams(
            dimension_semantics=("arbitrary",), vmem_limit_bytes=VMEM_LIMIT),
        name="ada",
    )(c, w, b)


def _inproj_kernel(x_ref, mod_ref, nw_ref, w_ref, wgk_ref, o_ref, gk_ref, h_scr):
    @pl.when(pl.program_id(1) == 0)
    def _():
        y = _rms_rows(x_ref[...]) * nw_ref[...]
        h = y * (1.0 + mod_ref[0, 1:2, :]) + mod_ref[0, 0:1, :]
        hb = h.astype(BF16)
        h_scr[...] = hb
        gk_ref[...] = _dot_nt(hb, wgk_ref[...])

    o_ref[...] = _dot_nt(h_scr[...], w_ref[...])


def _inproj(x2, mod, norm_w, w_t, seq):
    m, d = x2.shape
    n_mix = MAIN_WIDTH - 2 * d
    gate0 = n_mix + GK_RANK
    assert w_t.shape == (MAIN_WIDTH + GK_RANK, d) and n_mix % TN_PROJ == 0 and (2 * d) % TN_PROJ == 0
    assert gate0 % 16 == 0
    ng = 2 * d // TN_PROJ
    tiles_per_seq = seq // TM_PROJ

    def w_rows(i, j):
        return (pl.multiple_of(jnp.where(j < ng, gate0 + j * TN_PROJ, (j - ng) * TN_PROJ), 16), 0)

    return pl.pallas_call(
        _inproj_kernel,
        grid=(m // TM_PROJ, MAIN_WIDTH // TN_PROJ),
        in_specs=[
            pl.BlockSpec((TM_PROJ, d), lambda i, j: (i, 0)),
            pl.BlockSpec((1, N_MOD, d), lambda i, j: (i // tiles_per_seq, 0, 0)),
            pl.BlockSpec((1, d), lambda i, j: (0, 0)),
            pl.BlockSpec((pl.Element(TN_PROJ), pl.Element(d)), w_rows),
            pl.BlockSpec((pl.Element(LANES), pl.Element(d)), lambda i, j: (n_mix, 0)),
        ],
        out_specs=[
            pl.BlockSpec((TM_PROJ, TN_PROJ), lambda i, j: (i, j)),
            pl.BlockSpec((TM_PROJ, LANES), lambda i, j: (i, 0)),
        ],
        out_shape=[
            jax.ShapeDtypeStruct((m, MAIN_WIDTH), F32),
            jax.ShapeDtypeStruct((m, LANES), F32),
        ],
        scratch_shapes=[pltpu.VMEM((TM_PROJ, d), BF16)],
        compiler_params=pltpu.CompilerParams(
            dimension_semantics=("arbitrary", "arbitrary"), vmem_limit_bytes=VMEM_LIMIT),
        name="inproj",
    )(x2, mod, norm_w, w_t, w_t)


class _ScoreMasks:
    def __init__(self, c):
        sub = lax.broadcasted_iota(jnp.int32, (SUBLANES, c), 0)
        self.lane = lax.broadcasted_iota(jnp.int32, (SUBLANES, c), 1)
        in_tile = self.lane & (SUBLANES - 1)
        self.low_bit = sub ^ in_tile
        self.diag = jnp.where(in_tile < sub, self.lane >> int(math.log2(SUBLANES)), -1)

    def level(self, g):
        return (self.low_bit & -g) == g


def _group_mid_rows(p, g):
    c, dk = p.shape
    p3 = p.reshape(c // SUBLANES, SUBLANES, dk)
    sub = lax.broadcasted_iota(jnp.int32, p3.shape, 1)
    out = None
    for s in range(0, SUBLANES, 2 * g):
        part = jnp.broadcast_to(p3[:, s + g - 1:s + g, :], p3.shape)
        out = part if out is None else jnp.where(sub >= s, part, out)
    return out.reshape(c, dk)


def _chunk_attend(q, k, g2, v_heads, st_refs):
    c, width = q.shape
    heads = len(v_heads)
    dk = width // heads
    groups = c // SUBLANES
    hk = lambda a, h: a[:, h * dk:(h + 1) * dk]
    rows = lambda a, r: a[r * SUBLANES:(r + 1) * SUBLANES, :]
    row = lax.broadcasted_iota(jnp.int32, (c, width), 0)
    masks = _ScoreMasks(c)

    p = g2
    small = [[None] * groups for _ in range(heads)]
    g = c // 2
    plan = []
    while g >= SUBLANES:
        plan.append(g)
        g //= 2

    g = 1
    big_scores = {}
    while g < c:
        if g < SUBLANES:
            second = (row & g) != 0
            if g == 1:
                t = pltpu.roll(p, 1, axis=0)
                expo = jnp.where(second, p, 0.0)
            else:
                t = _group_mid_rows(p, g)
                expo = jnp.where(second, p, t - p)
            z = jnp.where(second, q, k) * jnp.exp2(expo)
            p = p + jnp.where(second, t, 0.0)
            zb = z.astype(BF16)
            lvl = None if g == 1 else masks.level(g)
            for h in range(heads):
                s = _dot_nt(hk(zb, h), hk(zb, h))
                for r in range(groups):
                    small[h][r] = rows(s, r) if g == 1 else jnp.where(lvl, rows(s, r), small[h][r])
        else:
            zs, ps = [], []
            for s0 in range(0, c, 2 * g):
                p1, p2 = p[s0:s0 + g, :], p[s0 + g:s0 + 2 * g, :]
                t = jnp.broadcast_to(p1[g - 1:g, :], (g, width))
                zs += [k[s0:s0 + g, :] * jnp.exp2(t - p1), q[s0 + g:s0 + 2 * g, :] * jnp.exp2(p2)]
                ps += [p1, p2 + t]
            zb = jnp.concatenate(zs, axis=0).astype(BF16)
            p = jnp.concatenate(ps, axis=0)
            zq = jnp.concatenate([zb[s0 + g:s0 + 2 * g, :] for s0 in range(0, c, 2 * g)], axis=0)
            big_scores[g] = [_dot_nt(hk(zq, h), hk(zb, h)) for h in range(heads)]
        g *= 2

    att = []
    for h in range(heads):
        out_rows = []
        for r in range(groups):
            first_row = r * SUBLANES
            acc, edge = None, 0
            for g in plan:
                if first_row & g:
                    blk = (first_row // (2 * g)) * g + (first_row % g)
                    piece = big_scores[g][h][blk:blk + SUBLANES, :]
                    acc = piece if acc is None else jnp.where(masks.lane < edge, acc, piece)
                    edge += g
            if acc is None:
                acc = jnp.zeros((SUBLANES, c), F32)
            else:
                acc = jnp.where(masks.lane < edge, acc, 0.0)
            out_rows.append(jnp.where(masks.diag == r, small[h][r], acc))
        att.append(jnp.concatenate(out_rows, axis=0).astype(BF16))

    qk = q * k
    q_dec = (q * jnp.exp2(p)).astype(BF16)
    p_last = p[c - 1:c, :]
    k_dec = (k * jnp.exp2(p_last - p)).astype(BF16)
    st_scale = jnp.exp2(p_last)
    outs = []
    for h in range(heads):
        v = v_heads[h]
        vb = v.astype(BF16)
        st = st_refs[h][...]
        o = _dot(att[h], vb)
        o = o + jnp.sum(hk(qk, h), axis=-1, keepdims=True) * v
        o = o + _dot_nt(hk(q_dec, h), st.astype(BF16))
        st_refs[h][...] = st * hk(st_scale, h) + _dot_tn(vb, hk(k_dec, h))
        outs.append(o)
    return outs


def _gated_head_norm(o, gate, w):
    return _rms_rows(o) * w * (gate * _sigmoid(gate))


def _mixers_kernel(qa_ref, fa_ref, ia_ref, ga_ref, qb_ref, kb_ref, vb_ref, gb_ref, gkl_ref,
                   lbp_ref, wgk_ref, bgk_ref, anw_ref, bnw_ref, ya_ref, yb_ref, sta_ref, stb_ref):
    @pl.when(pl.program_id(1) == 0)
    def _():
        sta_ref[...] = jnp.zeros_like(sta_ref)
        stb_ref[...] = jnp.zeros_like(stb_ref)

    lbp = lbp_ref[...]
    pe = jnp.exp(lbp - jnp.max(lbp, axis=0, keepdims=True))
    lb = pe[0:1, :] / jnp.sum(pe, axis=0, keepdims=True)
    anw, bnw = anw_ref[...], bnw_ref[...]
    a_cols = [slice(h * A_HEAD_DIM, (h + 1) * A_HEAD_DIM) for h in range(A_HEADS)]
    b_cols = [slice(h * B_VAL_DIM, (h + 1) * B_VAL_DIM) for h in range(B_HEADS)]
    st_refs = [sta_ref.at[h] for h in range(A_HEADS)] + [stb_ref.at[h] for h in range(B_HEADS)]

    for ci in range(CHUNKS_PER_STEP):
        rows = slice(ci * CHUNK, (ci + 1) * CHUNK)
        qa = qa_ref[rows, :]
        f = lb + (1.0 - lb) * _sigmoid(fa_ref[rows, :])
        z2 = (_dot(gkl_ref[rows, :].astype(BF16), wgk_ref[...]) + bgk_ref[...]) * LOG2E
        gk = (jnp.minimum(z2, 0.0) - jnp.log2(1.0 + jnp.exp2(-jnp.abs(z2)))) * (1.0 / GATE_LOGIT_NORMALIZER)

        q = jnp.concatenate([qa * _sigmoid(qa), qb_ref[rows, :] * (B_KEY_DIM ** -0.5)], axis=1)
        k = jnp.concatenate([1.0 - f, kb_ref[rows, :]], axis=1)
        g2 = jnp.concatenate([jnp.log2(f), gk], axis=1)
        v_heads = [ia_ref[rows, sl] for sl in a_cols] + [vb_ref[rows, sl] for sl in b_cols]
        outs = _chunk_attend(q, k, g2, v_heads, st_refs)

        for sl, o in zip(a_cols, outs[:A_HEADS]):
            ya_ref[rows, sl] = _gated_head_norm(o, ga_ref[rows, sl], anw).astype(ya_ref.dtype)
        for sl, o in zip(b_cols, outs[A_HEADS:]):
            yb_ref[rows, sl] = _gated_head_norm(o, gb_ref[rows, sl], bnw).astype(yb_ref.dtype)


def _mixers(proj, gk_low, lb_param, w_gk2p, b_gk2, a_norm_w, b_norm_w, bsz, seq):
    m = proj.shape[0]
    tm = CHUNK * CHUNKS_PER_STEP
    nc = seq // tm
    assert A_HEAD_DIM == B_KEY_DIM

    def col(c0, width):
        assert c0 % width == 0
        return pl.BlockSpec((tm, width), lambda b, c: (b * nc + c, c0 // width))

    const = lambda b, c: (0, 0)
    return pl.pallas_call(
        _mixers_kernel,
        grid=(bsz, nc),
        in_specs=[
            col(COL_QA, A_WIDTH), col(COL_FA, A_WIDTH), col(COL_IA, A_WIDTH), col(COL_GA, A_WIDTH),
            col(COL_QB, B_KEY_WIDTH), col(COL_KB, B_KEY_WIDTH),
            col(COL_VB, B_VAL_WIDTH), col(COL_GB, B_VAL_WIDTH),
            col(0, LANES),
            pl.BlockSpec(lb_param.shape, const),
            pl.BlockSpec((LANES, B_KEY_WIDTH), const),
            pl.BlockSpec((1, B_KEY_WIDTH), const),
            pl.BlockSpec((1, A_HEAD_DIM), const),
            pl.BlockSpec((1, B_VAL_DIM), const),
        ],
        out_specs=[col(0, A_WIDTH), col(0, B_VAL_WIDTH)],
        out_shape=[
            jax.ShapeDtypeStruct((m, A_WIDTH), BF16),
            jax.ShapeDtypeStruct((m, B_VAL_WIDTH), BF16),
        ],
        scratch_shapes=[
            pltpu.VMEM((A_HEADS, A_HEAD_DIM, A_HEAD_DIM), F32),
            pltpu.VMEM((B_HEADS, B_VAL_DIM, B_KEY_DIM), F32),
        ],
        compiler_params=pltpu.CompilerParams(
            dimension_semantics=("arbitrary", "arbitrary"), vmem_limit_bytes=VMEM_LIMIT),
        name="mixers",
    )(proj, proj, proj, proj, proj, proj, proj, proj, gk_low, lb_param, w_gk2p, b_gk2, a_norm_w, b_norm_w)


def _merge_kernel(ya_ref, yb_ref, ma_ref, mb_ref, x_ref, mod_ref, nw_ref,
                  wua_ref, wub_ref, wo_ref, x1_ref, h2_ref):
    ua = _dot(ya_ref[...], wua_ref[...])
    ub = _dot(yb_ref[...], wub_ref[...])
    merged = _sigmoid(ma_ref[...]) * ua + _sigmoid(mb_ref[...]) * ub
    out = _dot(merged.astype(BF16), wo_ref[...])
    x1 = x_ref[...] + mod_ref[0, 2:3, :] * out
    x1_ref[...] = x1
    y = _rms_rows(x1) * nw_ref[...]
    h2_ref[...] = (y * (1.0 + mod_ref[0, 4:5, :]) + mod_ref[0, 3:4, :]).astype(h2_ref.dtype)


def _merge(ya, yb, proj, x2, mod, norm_w, w_up_a, w_up_b, w_o, seq):
    m, d = x2.shape
    assert COL_MA % d == 0 and COL_MB % d == 0
    tiles_per_seq = seq // TM_MERGE
    row = lambda i: (i, 0)
    const = lambda i: (0, 0)
    return pl.pallas_call(
        _merge_kernel,
        grid=(m // TM_MERGE,),
        in_specs=[
            pl.BlockSpec((TM_MERGE, A_WIDTH), row),
            pl.BlockSpec((TM_MERGE, B_VAL_WIDTH), row),
            pl.BlockSpec((TM_MERGE, d), lambda i: (i, COL_MA // d)),
            pl.BlockSpec((TM_MERGE, d), lambda i: (i, COL_MB // d)),
            pl.BlockSpec((TM_MERGE, d), row),
            pl.BlockSpec((1, N_MOD, d), lambda i: (i // tiles_per_seq, 0, 0)),
            pl.BlockSpec((1, d), const),
            pl.BlockSpec(w_up_a.shape, const),
            pl.BlockSpec(w_up_b.shape, const),
            pl.BlockSpec(w_o.shape, const),
        ],
        out_specs=[pl.BlockSpec((TM_MERGE, d), row), pl.BlockSpec((TM_MERGE, d), row)],
        out_shape=[jax.ShapeDtypeStruct((m, d), F32), jax.ShapeDtypeStruct((m, d), BF16)],
        compiler_params=pltpu.CompilerParams(
            dimension_semantics=("arbitrary",), vmem_limit_bytes=VMEM_LIMIT),
        name="merge",
    )(ya, yb, proj, proj, x2, mod, norm_w, w_up_a, w_up_b, w_o)


def _ffn_up_kernel(h_ref, wg_ref, wu_ref, o_ref):
    h = h_ref[...]
    gate = _dot(h, wg_ref[...])
    up = _dot(h, wu_ref[...])
    o_ref[...] = (gate * _sigmoid(gate) * up).astype(o_ref.dtype)


def _ffn_up(h2, w_in):
    m, d = h2.shape
    hidden = w_in.shape[1] // 2
    nf = hidden // TF_FFN
    return pl.pallas_call(
        _ffn_up_kernel,
        grid=(m // TM_FFN_UP, nf),
        in_specs=[
            pl.BlockSpec((TM_FFN_UP, d), lambda i, j: (i, 0)),
            pl.BlockSpec((d, TF_FFN), lambda i, j: (0, j)),
            pl.BlockSpec((d, TF_FFN), lambda i, j: (0, j + nf)),
        ],
        out_specs=pl.BlockSpec((TM_FFN_UP, TF_FFN), lambda i, j: (i, j)),
        out_shape=jax.ShapeDtypeStruct((m, hidden), BF16),
        compiler_params=pltpu.CompilerParams(
            dimension_semantics=("arbitrary", "arbitrary"), vmem_limit_bytes=VMEM_LIMIT),
        name="ffn_up",
    )(h2, w_in, w_in)


def _ffn_down_kernel(a_ref, wo_ref, x1_ref, mod_ref, nw_ref, o_ref):
    x2 = x1_ref[...] + mod_ref[0, 5:6, :] * _dot(a_ref[...], wo_ref[...])
    o_ref[...] = _rms_rows(x2) * nw_ref[...]


def _ffn_down(act, w_out, x1, mod, norm_w, seq):
    m, d = x1.shape
    hidden = w_out.shape[0]
    tiles_per_seq = seq // TM_FFN_DOWN
    return pl.pallas_call(
        _ffn_down_kernel,
        grid=(m // TM_FFN_DOWN,),
        in_specs=[
            pl.BlockSpec((TM_FFN_DOWN, hidden), lambda i: (i, 0)),
            pl.BlockSpec((hidden, d), lambda i: (0, 0)),
            pl.BlockSpec((TM_FFN_DOWN, d), lambda i: (i, 0)),
            pl.BlockSpec((1, N_MOD, d), lambda i: (i // tiles_per_seq, 0, 0)),
            pl.BlockSpec((1, d), lambda i: (0, 0)),
        ],
        out_specs=pl.BlockSpec((TM_FFN_DOWN, d), lambda i: (i, 0)),
        out_shape=jax.ShapeDtypeStruct((m, d), F32),
        compiler_params=pltpu.CompilerParams(
            dimension_semantics=("arbitrary",), vmem_limit_bytes=VMEM_LIMIT),
        name="ffn_down",
    )(act, w_out, x1, mod, norm_w)


def kernel(x, c, w_ada, b_ada, norm1_w, w_in, w_gk2, b_gk2, lb_param, a_norm_w, b_norm_w,
           w_up_a, w_up_b, w_o, norm2_w, w_ffn_in, w_ffn_out, final_norm_w):
    bsz, seq, d = x.shape
    depth = w_in.shape[0]
    assert depth == 1 and d == D_MODEL and w_in.shape[2] == MAIN_WIDTH + GK_RANK
    assert seq % TM_PROJ == 0 and seq % (CHUNK * CHUNKS_PER_STEP) == 0 and seq % TM_MERGE == 0
    assert seq % TM_FFN_UP == 0 and seq % TM_FFN_DOWN == 0
    m = bsz * seq
    x2 = x.reshape(m, d)

    def layer0(w):
        return w.reshape(w.shape[1:])

    mod = _ada(c, layer0(w_ada), b_ada).reshape(bsz, N_MOD, d)

    w_in_t = jnp.swapaxes(layer0(w_in), 0, 1).astype(BF16)
    proj, gk_low = _inproj(x2, mod, norm1_w, w_in_t, seq)

    w_gk2p = jnp.pad(layer0(w_gk2).astype(BF16), ((0, LANES - GK_RANK), (0, 0)))
    ya, yb = _mixers(proj, gk_low, lb_param, w_gk2p, b_gk2, a_norm_w, b_norm_w, bsz, seq)

    x1, h2 = _merge(ya, yb, proj, x2, mod, norm2_w, layer0(w_up_a).astype(BF16),
                    layer0(w_up_b).astype(BF16), layer0(w_o).astype(BF16), seq)

    act = _ffn_up(h2, layer0(w_ffn_in).astype(BF16))
    out = _ffn_down(act, layer0(w_ffn_out).astype(BF16), x1, mod, final_norm_w.reshape(1, d), seq)
    return out.reshape(bsz, seq, d)
```

```python
import functools
import math

import jax
import jax.numpy as jnp
from jax import lax
from jax.experimental import pallas as pl
from jax.experimental.pallas import tpu as pltpu

F32 = jnp.float32
BF16 = jnp.bfloat16

D_MODEL = 2048
A_HEADS = 8
A_HEAD_DIM = 128
A_WIDTH = A_HEADS * A_HEAD_DIM
B_HEADS = 4
B_KEY_DIM = 128
B_VAL_DIM = 256
B_KEY_WIDTH = B_HEADS * B_KEY_DIM
B_VAL_WIDTH = B_HEADS * B_VAL_DIM
GK_RANK = 16
GATE_LOGIT_NORMALIZER = 16.0
N_MOD = 6
EPS = 1e-6
LOG2E = math.log2(math.e)

LANES = 128
SUBLANES = 8
VMEM_LIMIT = 56 * 1024 * 1024

MAIN_WIDTH = 4 * A_WIDTH + 2 * B_KEY_WIDTH + 2 * B_VAL_WIDTH + 2 * D_MODEL
COL_MA, COL_MB = 0, D_MODEL
COL_QA = 2 * D_MODEL
COL_FA, COL_IA, COL_GA = COL_QA + A_WIDTH, COL_QA + 2 * A_WIDTH, COL_QA + 3 * A_WIDTH
COL_QB = COL_QA + 4 * A_WIDTH
COL_KB = COL_QB + B_KEY_WIDTH
COL_VB = COL_KB + B_KEY_WIDTH
COL_GB = COL_VB + B_VAL_WIDTH

CHUNK = 128
CHUNKS_PER_STEP = 4
TM_PROJ = 1024
TN_PROJ = 1024
TM_MERGE = 256
TM_FFN_UP = 2048
TF_FFN = 512
TM_FFN_DOWN = 512
TN_ADA = 1024


def _sigmoid(x):
    return 1.0 / (1.0 + jnp.exp2(x * (-LOG2E)))


def _rms_rows(x):
    return x * lax.rsqrt(jnp.mean(x * x, axis=-1, keepdims=True) + EPS)


def _dot(a, b):
    return jnp.dot(a, b, preferred_element_type=F32)


def _dot_nt(a, b):
    return lax.dot_general(a, b, (((1,), (1,)), ((), ())), preferred_element_type=F32)


def _dot_tn(a, b):
    return lax.dot_general(a, b, (((0,), (0,)), ((), ())), preferred_element_type=F32)


def _ada_kernel(c_ref, w_ref, b_ref, o_ref):
    c = c_ref[...]
    cond = (c * _sigmoid(c)).astype(BF16)
    o_ref[...] = _dot(cond, w_ref[...].astype(BF16)) + b_ref[...]


def _ada(c, w, b):
    bsz, d = c.shape
    n = w.shape[1]
    return pl.pallas_call(
        _ada_kernel,
        grid=(n // TN_ADA,),
        in_specs=[
            pl.BlockSpec((bsz, d), lambda j: (0, 0)),
            pl.BlockSpec((d, TN_ADA), lambda j: (0, j)),
            pl.BlockSpec((1, TN_ADA), lambda j: (0, j)),
        ],
        out_specs=pl.BlockSpec((bsz, TN_ADA), lambda j: (0, j)),
        out_shape=jax.ShapeDtypeStruct((bsz, n), F32),
        compiler_params=pltpu.CompilerParams(
            dimension_semantics=("arbitrary",), vmem_limit_bytes=VMEM_LIMIT),
        name="ada",
    )(c, w, b)


def _inproj_kernel(x_ref, mod_ref, nw_ref, w_ref, wgk_ref, o_ref, gk_ref, h_scr):
    @pl.when(pl.program_id(1) == 0)
    def _():
        y = _rms_rows(x_ref[...]) * nw_ref[...]
        h = y * (1.0 + mod_ref[0, 1:2, :]) + mod_ref[0, 0:1, :]
        hb = h.astype(BF16)
        h_scr[...] = hb
        gk_ref[...] = _dot_nt(hb, wgk_ref[...])

    o_ref[...] = _dot_nt(h_scr[...], w_ref[...])


def _inproj(x2, mod, norm_w, w_t, seq):
    m, d = x2.shape
    n_mix = MAIN_WIDTH - 2 * d
    gate0 = n_mix + GK_RANK
    assert w_t.shape == (MAIN_WIDTH + GK_RANK, d) and n_mix % TN_PROJ == 0 and (2 * d) % TN_PROJ == 0
    assert gate0 % 16 == 0
    ng = 2 * d // TN_PROJ
    tiles_per_seq = seq // TM_PROJ

    def w_rows(i, j):
        return (pl.multiple_of(jnp.where(j < ng, gate0 + j * TN_PROJ, (j - ng) * TN_PROJ), 16), 0)

    return pl.pallas_call(
        _inproj_kernel,
        grid=(m // TM_PROJ, MAIN_WIDTH // TN_PROJ),
        in_specs=[
            pl.BlockSpec((TM_PROJ, d), lambda i, j: (i, 0)),
            pl.BlockSpec((1, N_MOD, d), lambda i, j: (i // tiles_per_seq, 0, 0)),
            pl.BlockSpec((1, d), lambda i, j: (0, 0)),
            pl.BlockSpec((pl.Element(TN_PROJ), pl.Element(d)), w_rows),
            pl.BlockSpec((pl.Element(LANES), pl.Element(d)), lambda i, j: (n_mix, 0)),
        ],
        out_specs=[
            pl.BlockSpec((TM_PROJ, TN_PROJ), lambda i, j: (i, j)),
            pl.BlockSpec((TM_PROJ, LANES), lambda i, j: (i, 0)),
        ],
        out_shape=[
            jax.ShapeDtypeStruct((m, MAIN_WIDTH), F32),
            jax.ShapeDtypeStruct((m, LANES), F32),
        ],
        scratch_shapes=[pltpu.VMEM((TM_PROJ, d), BF16)],
        compiler_params=pltpu.CompilerParams(
            dimension_semantics=("arbitrary", "arbitrary"), vmem_limit_bytes=VMEM_LIMIT),
        name="inproj",
    )(x2, mod, norm_w, w_t, w_t)


class _ScoreMasks:
    def __init__(self, c):
        sub = lax.broadcasted_iota(jnp.int32, (SUBLANES, c), 0)
        self.lane = lax.broadcasted_iota(jnp.int32, (SUBLANES, c), 1)
        in_tile = self.lane & (SUBLANES - 1)
        self.low_bit = sub ^ in_tile
        self.diag = jnp.where(in_tile < sub, self.lane >> int(math.log2(SUBLANES)), -1)

    def level(self, g):
        return (self.low_bit & -g) == g


def _group_mid_rows(p, g):
    c, dk = p.shape
    p3 = p.reshape(c // SUBLANES, SUBLANES, dk)
    sub = lax.broadcasted_iota(jnp.int32, p3.shape, 1)
    out = None
    for s in range(0, SUBLANES, 2 * g):
        part = jnp.broadcast_to(p3[:, s + g - 1:s + g, :], p3.shape)
        out = part if out is None else jnp.where(sub >= s, part, out)
    return out.reshape(c, dk)


def _chunk_attend(q, k, g2, v_heads, st_refs):
    c, width = q.shape
    heads = len(v_heads)
    dk = width // heads
    groups = c // SUBLANES
    hk = lambda a, h: a[:, h * dk:(h + 1) * dk]
    rows = lambda a, r: a[r * SUBLANES:(r + 1) * SUBLANES, :]
    row = lax.broadcasted_iota(jnp.int32, (c, width), 0)
    masks = _ScoreMasks(c)

    p = g2
    small = [[None] * groups for _ in range(heads)]
    g = c // 2
    plan = []
    while g >= SUBLANES:
        plan.append(g)
        g //= 2

    g = 1
    big_scores = {}
    while g < c:
        if g < SUBLANES:
            second = (row & g) != 0
            if g == 1:
                t = pltpu.roll(p, 1, axis=0)
                expo = jnp.where(second, p, 0.0)
            else:
                t = _group_mid_rows(p, g)
                expo = jnp.where(second, p, t - p)
            z = jnp.where(second, q, k) * jnp.exp2(expo)
            p = p + jnp.where(second, t, 0.0)
            zb = z.astype(BF16)
            lvl = None if g == 1 else masks.level(g)
            for h in range(heads):
                s = _dot_nt(hk(zb, h), hk(zb, h))
                for r in range(groups):
                    small[h][r] = rows(s, r) if g == 1 else jnp.where(lvl, rows(s, r), small[h][r])
        else:
            zs, ps = [], []
            for s0 in range(0, c, 2 * g):
                p1, p2 = p[s0:s0 + g, :], p[s0 + g:s0 + 2 * g, :]
                t = jnp.broadcast_to(p1[g - 1:g, :], (g, width))
                zs += [k[s0:s0 + g, :] * jnp.exp2(t - p1), q[s0 + g:s0 + 2 * g, :] * jnp.exp2(p2)]
                ps += [p1, p2 + t]
            zb = jnp.concatenate(zs, axis=0).astype(BF16)
            p = jnp.concatenate(ps, axis=0)
            zq = jnp.concatenate([zb[s0 + g:s0 + 2 * g, :] for s0 in range(0, c, 2 * g)], axis=0)
            big_scores[g] = [_dot_nt(hk(zq, h), hk(zb, h)) for h in range(heads)]
        g *= 2

    att = []
    for h in range(heads):
        out_rows = []
        for r in range(groups):
            first_row = r * SUBLANES
            acc, edge = None, 0
            for g in plan:
                if first_row & g:
                    blk = (first_row // (2 * g)) * g + (first_row % g)
                    piece = big_scores[g][h][blk:blk + SUBLANES, :]
                    acc = piece if acc is None else jnp.where(masks.lane < edge, acc, piece)
                    edge += g
            if acc is None:
                acc = jnp.zeros((SUBLANES, c), F32)
            else:
                acc = jnp.where(masks.lane < edge, acc, 0.0)
            out_rows.append(jnp.where(masks.diag == r, small[h][r], acc))
        att.append(jnp.concatenate(out_rows, axis=0).astype(BF16))

    qk = q * k
    q_dec = (q * jnp.exp2(p)).astype(BF16)
    p_last = p[c - 1:c, :]
    k_dec = (k * jnp.exp2(p_last - p)).astype(BF16)
    st_scale = jnp.exp2(p_last)
    outs = []
    for h in range(heads):
        v = v_heads[h]
        vb = v.astype(BF16)
        st = st_refs[h][...]
        o = _dot(att[h], vb)
        o = o + jnp.sum(hk(qk, h), axis=-1, keepdims=True) * v
        o = o + _dot_nt(hk(q_dec, h), st.astype(BF16))
        st_refs[h][...] = st * hk(st_scale, h) + _dot_tn(vb, hk(k_dec, h))
        outs.append(o)
    return outs


def _gated_head_norm(o, gate, w):
    return _rms_rows(o) * w * (gate * _sigmoid(gate))


def _mixers_kernel(qa_ref, fa_ref, ia_ref, ga_ref, qb_ref, kb_ref, vb_ref, gb_ref, gkl_ref,
                   lbp_ref, wgk_ref, bgk_ref, anw_ref, bnw_ref, wua_ref, wub_ref, wo_ref,
                   ya_ref, yb_ref, wua_bf_ref, wub_bf_ref, wo_bf_ref, sta_ref, stb_ref):
    @pl.when(pl.program_id(1) == 0)
    def _():
        sta_ref[...] = jnp.zeros_like(sta_ref)
        stb_ref[...] = jnp.zeros_like(stb_ref)

    for src, dst in ((wua_ref, wua_bf_ref), (wub_ref, wub_bf_ref), (wo_ref, wo_bf_ref)):
        dst[...] = src[...].astype(dst.dtype)

    lbp = lbp_ref[...]
    pe = jnp.exp(lbp - jnp.max(lbp, axis=0, keepdims=True))
    lb = pe[0:1, :] / jnp.sum(pe, axis=0, keepdims=True)
    anw, bnw = anw_ref[...], bnw_ref[...]
    a_cols = [slice(h * A_HEAD_DIM, (h + 1) * A_HEAD_DIM) for h in range(A_HEADS)]
    b_cols = [slice(h * B_VAL_DIM, (h + 1) * B_VAL_DIM) for h in range(B_HEADS)]
    st_refs = [sta_ref.at[h] for h in range(A_HEADS)] + [stb_ref.at[h] for h in range(B_HEADS)]

    for ci in range(CHUNKS_PER_STEP):
        rows = slice(ci * CHUNK, (ci + 1) * CHUNK)
        qa = qa_ref[rows, :]
        f = lb + (1.0 - lb) * _sigmoid(fa_ref[rows, :])
        z2 = (_dot(gkl_ref[rows, :].astype(BF16), wgk_ref[...]) + bgk_ref[...]) * LOG2E
        gk = (jnp.minimum(z2, 0.0) - jnp.log2(1.0 + jnp.exp2(-jnp.abs(z2)))) * (1.0 / GATE_LOGIT_NORMALIZER)

        q = jnp.concatenate([qa * _sigmoid(qa), qb_ref[rows, :] * (B_KEY_DIM ** -0.5)], axis=1)
        k = jnp.concatenate([1.0 - f, kb_ref[rows, :]], axis=1)
        g2 = jnp.concatenate([jnp.log2(f), gk], axis=1)
        v_heads = [ia_ref[rows, sl] for sl in a_cols] + [vb_ref[rows, sl] for sl in b_cols]
        outs = _chunk_attend(q, k, g2, v_heads, st_refs)

        for sl, o in zip(a_cols, outs[:A_HEADS]):
            ya_ref[rows, sl] = _gated_head_norm(o, ga_ref[rows, sl], anw).astype(ya_ref.dtype)
        for sl, o in zip(b_cols, outs[A_HEADS:]):
            yb_ref[rows, sl] = _gated_head_norm(o, gb_ref[rows, sl], bnw).astype(yb_ref.dtype)


def _mixers(proj, gk_low, lb_param, w_gk2p, b_gk2, a_norm_w, b_norm_w, merge_weights, bsz, seq):
    m = proj.shape[0]
    tm = CHUNK * CHUNKS_PER_STEP
    nc = seq // tm
    steps = bsz * nc
    assert A_HEAD_DIM == B_KEY_DIM
    assert all(w.shape[0] % (steps * 16) == 0 for w in merge_weights)

    def share(w):
        return pl.BlockSpec((w.shape[0] // steps, w.shape[1]), lambda b, c: (b * nc + c, 0))

    def col(c0, width):
        assert c0 % width == 0
        return pl.BlockSpec((tm, width), lambda b, c: (b * nc + c, c0 // width))

    const = lambda b, c: (0, 0)
    return pl.pallas_call(
        _mixers_kernel,
        grid=(bsz, nc),
        in_specs=[
            col(COL_QA, A_WIDTH), col(COL_FA, A_WIDTH), col(COL_IA, A_WIDTH), col(COL_GA, A_WIDTH),
            col(COL_QB, B_KEY_WIDTH), col(COL_KB, B_KEY_WIDTH),
            col(COL_VB, B_VAL_WIDTH), col(COL_GB, B_VAL_WIDTH),
            col(0, LANES),
            pl.BlockSpec(lb_param.shape, const),
            pl.BlockSpec((LANES, B_KEY_WIDTH), const),
            pl.BlockSpec((1, B_KEY_WIDTH), const),
            pl.BlockSpec((1, A_HEAD_DIM), const),
            pl.BlockSpec((1, B_VAL_DIM), const),
        ] + [share(w) for w in merge_weights],
        out_specs=[col(0, A_WIDTH), col(0, B_VAL_WIDTH)] + [share(w) for w in merge_weights],
        out_shape=[
            jax.ShapeDtypeStruct((m, A_WIDTH), BF16),
            jax.ShapeDtypeStruct((m, B_VAL_WIDTH), BF16),
        ] + [jax.ShapeDtypeStruct(w.shape, BF16) for w in merge_weights],
        scratch_shapes=[
            pltpu.VMEM((A_HEADS, A_HEAD_DIM, A_HEAD_DIM), F32),
            pltpu.VMEM((B_HEADS, B_VAL_DIM, B_KEY_DIM), F32),
        ],
        compiler_params=pltpu.CompilerParams(
            dimension_semantics=("arbitrary", "arbitrary"), vmem_limit_bytes=VMEM_LIMIT),
        name="mixers",
    )(proj, proj, proj, proj, proj, proj, proj, proj, gk_low, lb_param, w_gk2p, b_gk2, a_norm_w, b_norm_w,
      *merge_weights)


def _merge_kernel(ya_ref, yb_ref, ma_ref, mb_ref, x_ref, mod_ref, nw_ref,
                  wua_ref, wub_ref, wo_ref, x1_ref, h2_ref):
    ua = _dot(ya_ref[...], wua_ref[...])
    ub = _dot(yb_ref[...], wub_ref[...])
    merged = _sigmoid(ma_ref[...]) * ua + _sigmoid(mb_ref[...]) * ub
    out = _dot(merged.astype(BF16), wo_ref[...])
    x1 = x_ref[...] + mod_ref[0, 2:3, :] * out
    x1_ref[...] = x1
    y = _rms_rows(x1) * nw_ref[...]
    h2_ref[...] = (y * (1.0 + mod_ref[0, 4:5, :]) + mod_ref[0, 3:4, :]).astype(h2_ref.dtype)


def _merge(ya, yb, proj, x2, mod, norm_w, w_up_a, w_up_b, w_o, seq):
    m, d = x2.shape
    assert COL_MA % d == 0 and COL_MB % d == 0
    tiles_per_seq = seq // TM_MERGE
    row = lambda i: (i, 0)
    const = lambda i: (0, 0)
    return pl.pallas_call(
        _merge_kernel,
        grid=(m // TM_MERGE,),
        in_specs=[
            pl.BlockSpec((TM_MERGE, A_WIDTH), row),
            pl.BlockSpec((TM_MERGE, B_VAL_WIDTH), row),
            pl.BlockSpec((TM_MERGE, d), lambda i: (i, COL_MA // d)),
            pl.BlockSpec((TM_MERGE, d), lambda i: (i, COL_MB // d)),
            pl.BlockSpec((TM_MERGE, d), row),
            pl.BlockSpec((1, N_MOD, d), lambda i: (i // tiles_per_seq, 0, 0)),
            pl.BlockSpec((1, d), const),
            pl.BlockSpec(w_up_a.shape, const),
            pl.BlockSpec(w_up_b.shape, const),
            pl.BlockSpec(w_o.shape, const),
        ],
        out_specs=[pl.BlockSpec((TM_MERGE, d), row), pl.BlockSpec((TM_MERGE, d), row)],
        out_shape=[jax.ShapeDtypeStruct((m, d), F32), jax.ShapeDtypeStruct((m, d), BF16)],
        compiler_params=pltpu.CompilerParams(
            dimension_semantics=("arbitrary",), vmem_limit_bytes=VMEM_LIMIT),
        name="merge",
    )(ya, yb, proj, proj, x2, mod, norm_w, w_up_a, w_up_b, w_o)


def _ffn_up_kernel(h_ref, wg_ref, wu_ref, wdown_ref, o_ref, wdown_bf_ref):
    h = h_ref[...]
    gate = _dot(h, wg_ref[...].astype(BF16))
    up = _dot(h, wu_ref[...].astype(BF16))
    o_ref[...] = (gate * _sigmoid(gate) * up).astype(o_ref.dtype)
    wdown_bf_ref[...] = wdown_ref[...].astype(wdown_bf_ref.dtype)


def _ffn_up(h2, w_in, w_down):
    m, d = h2.shape
    hidden = w_in.shape[1] // 2
    nf = hidden // TF_FFN
    steps = (m // TM_FFN_UP) * nf
    assert w_down.shape == (hidden, d) and hidden % (steps * 16) == 0
    share = hidden // steps
    return pl.pallas_call(
        _ffn_up_kernel,
        grid=(m // TM_FFN_UP, nf),
        in_specs=[
            pl.BlockSpec((TM_FFN_UP, d), lambda i, j: (i, 0)),
            pl.BlockSpec((d, TF_FFN), lambda i, j: (0, j)),
            pl.BlockSpec((d, TF_FFN), lambda i, j: (0, j + nf)),
            pl.BlockSpec((share, d), lambda i, j: (i * nf + j, 0)),
        ],
        out_specs=[
            pl.BlockSpec((TM_FFN_UP, TF_FFN), lambda i, j: (i, j)),
            pl.BlockSpec((share, d), lambda i, j: (i * nf + j, 0)),
        ],
        out_shape=[
            jax.ShapeDtypeStruct((m, hidden), BF16),
            jax.ShapeDtypeStruct((hidden, d), BF16),
        ],
        compiler_params=pltpu.CompilerParams(
            dimension_semantics=("arbitrary", "arbitrary"), vmem_limit_bytes=VMEM_LIMIT),
        name="ffn_up",
    )(h2, w_in, w_in, w_down)


def _ffn_down_kernel(a_ref, wo_ref, x1_ref, mod_ref, nw_ref, o_ref):
    x2 = x1_ref[...] + mod_ref[0, 5:6, :] * _dot(a_ref[...], wo_ref[...])
    o_ref[...] = _rms_rows(x2) * nw_ref[...]


def _ffn_down(act, w_out, x1, mod, norm_w, seq):
    m, d = x1.shape
    hidden = w_out.shape[0]
    tiles_per_seq = seq // TM_FFN_DOWN
    return pl.pallas_call(
        _ffn_down_kernel,
        grid=(m // TM_FFN_DOWN,),
        in_specs=[
            pl.BlockSpec((TM_FFN_DOWN, hidden), lambda i: (i, 0)),
            pl.BlockSpec((hidden, d), lambda i: (0, 0)),
            pl.BlockSpec((TM_FFN_DOWN, d), lambda i: (i, 0)),
            pl.BlockSpec((1, N_MOD, d), lambda i: (i // tiles_per_seq, 0, 0)),
            pl.BlockSpec((1, d), lambda i: (0, 0)),
        ],
        out_specs=pl.BlockSpec((TM_FFN_DOWN, d), lambda i: (i, 0)),
        out_shape=jax.ShapeDtypeStruct((m, d), F32),
        compiler_params=pltpu.CompilerParams(
            dimension_semantics=("arbitrary",), vmem_limit_bytes=VMEM_LIMIT),
        name="ffn_down",
    )(act, w_out, x1, mod, norm_w)


def kernel(x, c, w_ada, b_ada, norm1_w, w_in, w_gk2, b_gk2, lb_param, a_norm_w, b_norm_w,
           w_up_a, w_up_b, w_o, norm2_w, w_ffn_in, w_ffn_out, final_norm_w):
    bsz, seq, d = x.shape
    depth = w_in.shape[0]
    assert depth == 1 and d == D_MODEL and w_in.shape[2] == MAIN_WIDTH + GK_RANK
    assert seq % TM_PROJ == 0 and seq % (CHUNK * CHUNKS_PER_STEP) == 0 and seq % TM_MERGE == 0
    assert seq % TM_FFN_UP == 0 and seq % TM_FFN_DOWN == 0
    m = bsz * seq
    x2 = x.reshape(m, d)

    def layer0(w):
        return w.reshape(w.shape[1:])

    mod = _ada(c, layer0(w_ada), b_ada).reshape(bsz, N_MOD, d)

    w_in_t = jnp.swapaxes(layer0(w_in), 0, 1).astype(BF16)
    proj, gk_low = _inproj(x2, mod, norm1_w, w_in_t, seq)

    w_gk2p = jnp.pad(layer0(w_gk2).astype(BF16), ((0, LANES - GK_RANK), (0, 0)))
    ya, yb, wua, wub, wo = _mixers(proj, gk_low, lb_param, w_gk2p, b_gk2, a_norm_w, b_norm_w,
                                   (layer0(w_up_a), layer0(w_up_b), layer0(w_o)), bsz, seq)

    x1, h2 = _merge(ya, yb, proj, x2, mod, norm2_w, wua, wub, wo, seq)

    act, w_down = _ffn_up(h2, layer0(w_ffn_in), layer0(w_ffn_out))
    out = _ffn_down(act, w_down, x1, mod, final_norm_w.reshape(1, d), seq)
    return out.reshape(bsz, seq, d)
```

```python
import functools
import math

import jax
import jax.numpy as jnp
from jax import lax
from jax.experimental import pallas as pl
from jax.experimental.pallas import tpu as pltpu

F32 = jnp.float32
BF16 = jnp.bfloat16

D_MODEL = 2048
A_HEADS = 8
A_HEAD_DIM = 128
A_WIDTH = A_HEADS * A_HEAD_DIM
B_HEADS = 4
B_KEY_DIM = 128
B_VAL_DIM = 256
B_KEY_WIDTH = B_HEADS * B_KEY_DIM
B_VAL_WIDTH = B_HEADS * B_VAL_DIM
GK_RANK = 16
GATE_LOGIT_NORMALIZER = 16.0
N_MOD = 6
EPS = 1e-6
LOG2E = math.log2(math.e)

LANES = 128
SUBLANES = 8
VMEM_LIMIT = 56 * 1024 * 1024

MAIN_WIDTH = 4 * A_WIDTH + 2 * B_KEY_WIDTH + 2 * B_VAL_WIDTH + 2 * D_MODEL
COL_MA, COL_MB = 0, D_MODEL
COL_QA = 2 * D_MODEL
COL_FA, COL_IA, COL_GA = COL_QA + A_WIDTH, COL_QA + 2 * A_WIDTH, COL_QA + 3 * A_WIDTH
COL_QB = COL_QA + 4 * A_WIDTH
COL_KB = COL_QB + B_KEY_WIDTH
COL_VB = COL_KB + B_KEY_WIDTH
COL_GB = COL_VB + B_VAL_WIDTH

CHUNK = 128
CHUNKS_PER_STEP = 2
TM_PROJ = 1024
TN_PROJ = 1024
TM_MERGE = 256
TM_FFN_UP = 2048
TF_FFN = 512
FFN_UP_SUBTILE = 256
FFN_UP_SUBROWS = 1024
TM_FFN_DOWN = 512
TN_ADA = 1024


def _sigmoid(x):
    return 1.0 / (1.0 + jnp.exp2(x * (-LOG2E)))


def _rms_rows(x):
    return x * lax.rsqrt(jnp.mean(x * x, axis=-1, keepdims=True) + EPS)


def _dot(a, b):
    return jnp.dot(a, b, preferred_element_type=F32)


def _dot_nt(a, b):
    return lax.dot_general(a, b, (((1,), (1,)), ((), ())), preferred_element_type=F32)


def _dot_tn(a, b):
    return lax.dot_general(a, b, (((0,), (0,)), ((), ())), preferred_element_type=F32)


def _ada_kernel(c_ref, w_ref, b_ref, o_ref):
    c = c_ref[...]
    cond = (c * _sigmoid(c)).astype(BF16)
    o_ref[...] = _dot(cond, w_ref[...].astype(BF16)) + b_ref[...]


def _ada(c, w, b):
    bsz, d = c.shape
    n = w.shape[1]
    return pl.pallas_call(
        _ada_kernel,
        grid=(n // TN_ADA,),
        in_specs=[
            pl.BlockSpec((bsz, d), lambda j: (0, 0)),
            pl.BlockSpec((d, TN_ADA), lambda j: (0, j)),
            pl.BlockSpec((1, TN_ADA), lambda j: (0, j)),
        ],
        out_specs=pl.BlockSpec((bsz, TN_ADA), lambda j: (0, j)),
        out_shape=jax.ShapeDtypeStruct((bsz, n), F32),
        compiler_params=pltpu.CompilerParams(
            dimension_semantics=("arbitrary",), vmem_limit_bytes=VMEM_LIMIT),
        name="ada",
    )(c, w, b)


def _inproj_kernel(x_ref, mod_ref, nw_ref, w_ref, wgk_ref, o_ref, gk_ref, h_scr):
    @pl.when(pl.program_id(1) == 0)
    def _():
        scale = nw_ref[...] * (1.0 + mod_ref[0, 1:2, :])
        hb = (_rms_rows(x_ref[...]) * scale + mod_ref[0, 0:1, :]).astype(BF16)
        h_scr[...] = hb
        gk_ref[...] = _dot_nt(hb, wgk_ref[...])

    o_ref[...] = _dot_nt(h_scr[...], w_ref[...])


def _inproj(x2, mod, norm_w, w_t, seq):
    m, d = x2.shape
    n_mix = MAIN_WIDTH - 2 * d
    gate0 = n_mix + GK_RANK
    assert w_t.shape == (MAIN_WIDTH + GK_RANK, d) and n_mix % TN_PROJ == 0 and (2 * d) % TN_PROJ == 0
    assert gate0 % 16 == 0
    ng = 2 * d // TN_PROJ
    tiles_per_seq = seq // TM_PROJ

    def w_rows(i, j):
        return (pl.multiple_of(jnp.where(j < ng, gate0 + j * TN_PROJ, (j - ng) * TN_PROJ), 16), 0)

    return pl.pallas_call(
        _inproj_kernel,
        grid=(m // TM_PROJ, MAIN_WIDTH // TN_PROJ),
        in_specs=[
            pl.BlockSpec((TM_PROJ, d), lambda i, j: (i, 0)),
            pl.BlockSpec((1, N_MOD, d), lambda i, j: (i // tiles_per_seq, 0, 0)),
            pl.BlockSpec((1, d), lambda i, j: (0, 0)),
            pl.BlockSpec((pl.Element(TN_PROJ), pl.Element(d)), w_rows),
            pl.BlockSpec((pl.Element(LANES), pl.Element(d)), lambda i, j: (n_mix, 0)),
        ],
        out_specs=[
            pl.BlockSpec((TM_PROJ, TN_PROJ), lambda i, j: (i, j)),
            pl.BlockSpec((TM_PROJ, LANES), lambda i, j: (i, 0)),
        ],
        out_shape=[
            jax.ShapeDtypeStruct((m, MAIN_WIDTH), F32),
            jax.ShapeDtypeStruct((m, LANES), F32),
        ],
        scratch_shapes=[pltpu.VMEM((TM_PROJ, d), BF16)],
        compiler_params=pltpu.CompilerParams(
            dimension_semantics=("arbitrary", "arbitrary"), vmem_limit_bytes=VMEM_LIMIT),
        name="inproj",
    )(x2, mod, norm_w, w_t, w_t)


class _ScoreMasks:
    def __init__(self, c):
        sub = lax.broadcasted_iota(jnp.int32, (SUBLANES, c), 0)
        self.lane = lax.broadcasted_iota(jnp.int32, (SUBLANES, c), 1)
        in_tile = self.lane & (SUBLANES - 1)
        self.low_bit = sub ^ in_tile
        self.diag = jnp.where(in_tile < sub, self.lane >> int(math.log2(SUBLANES)), -1)

    def level(self, g):
        return (self.low_bit & -g) == g


def _group_mid_rows(p, g):
    c, dk = p.shape
    p3 = p.reshape(c // SUBLANES, SUBLANES, dk)
    sub = lax.broadcasted_iota(jnp.int32, p3.shape, 1)
    out = None
    for s in range(0, SUBLANES, 2 * g):
        part = jnp.broadcast_to(p3[:, s + g - 1:s + g, :], p3.shape)
        out = part if out is None else jnp.where(sub >= s, part, out)
    return out.reshape(c, dk)


def _chunk_attend(q, k, g2, v_heads, st_refs):
    c, width = q.shape
    heads = len(v_heads)
    dk = width // heads
    groups = c // SUBLANES
    hk = lambda a, h: a[:, h * dk:(h + 1) * dk]
    rows = lambda a, r: a[r * SUBLANES:(r + 1) * SUBLANES, :]
    row = lax.broadcasted_iota(jnp.int32, (c, width), 0)
    masks = _ScoreMasks(c)

    p = g2
    small = [[None] * groups for _ in range(heads)]
    g = c // 2
    plan = []
    while g >= SUBLANES:
        plan.append(g)
        g //= 2

    g = 1
    big_scores = {}
    while g < c:
        if g < SUBLANES:
            second = (row & g) != 0
            if g == 1:
                t = pltpu.roll(p, 1, axis=0)
                expo = jnp.where(second, p, 0.0)
            else:
                t = _group_mid_rows(p, g)
                expo = jnp.where(second, p, t - p)
            z = jnp.where(second, q, k) * jnp.exp2(expo)
            p = p + jnp.where(second, t, 0.0)
            zb = z.astype(BF16)
            lvl = None if g == 1 else masks.level(g)
            for h in range(heads):
                s = _dot_nt(hk(zb, h), hk(zb, h))
                for r in range(groups):
                    small[h][r] = rows(s, r) if g == 1 else jnp.where(lvl, rows(s, r), small[h][r])
        else:
            zs, ps = [], []
            for s0 in range(0, c, 2 * g):
                p1, p2 = p[s0:s0 + g, :], p[s0 + g:s0 + 2 * g, :]
                t = jnp.broadcast_to(p1[g - 1:g, :], (g, width))
                zs += [k[s0:s0 + g, :] * jnp.exp2(t - p1), q[s0 + g:s0 + 2 * g, :] * jnp.exp2(p2)]
                ps += [p1, p2 + t]
            zb = jnp.concatenate(zs, axis=0).astype(BF16)
            p = jnp.concatenate(ps, axis=0)
            zq = jnp.concatenate([zb[s0 + g:s0 + 2 * g, :] for s0 in range(0, c, 2 * g)], axis=0)
            big_scores[g] = [_dot_nt(hk(zq, h), hk(zb, h)) for h in range(heads)]
        g *= 2


    att = []
    for h in range(heads):
        out_rows = []
        for r in range(groups):
            first_row = r * SUBLANES
            acc, edge = None, 0
            for g in plan:
                if first_row & g:
                    blk = (first_row // (2 * g)) * g + (first_row % g)
                    piece = big_scores[g][h][blk:blk + SUBLANES, :]
                    acc = piece if acc is None else jnp.where(masks.lane < edge, acc, piece)
                    edge += g
            if acc is None:
                acc = jnp.zeros((SUBLANES, c), F32)
            else:
                acc = jnp.where(masks.lane < edge, acc, 0.0)
            out_rows.append(jnp.where(masks.diag == r, small[h][r], acc))
        att.append(jnp.concatenate(out_rows, axis=0).astype(BF16))

    qk = q * k
    q_dec = (q * jnp.exp2(p)).astype(BF16)
    p_last = p[c - 1:c, :]
    k_dec = (k * jnp.exp2(p_last - p)).astype(BF16)
    st_scale = jnp.exp2(p_last)
    outs = []
    for h in range(heads):
        v = v_heads[h]
        vb = v.astype(BF16)
        st = st_refs[h][...]
        o = _dot(att[h], vb)
        o = o + jnp.sum(hk(qk, h), axis=-1, keepdims=True) * v
        o = o + _dot_nt(hk(q_dec, h), st.astype(BF16))
        st_refs[h][...] = st * hk(st_scale, h) + _dot_tn(vb, hk(k_dec, h))
        outs.append(o)
    return outs


def _gated_head_norm(o, gate, w):
    return _rms_rows(o) * w * (gate * _sigmoid(gate))


def _mixers_kernel(qa_ref, fa_ref, ia_ref, ga_ref, qb_ref, kb_ref, vb_ref, gb_ref, gkl_ref,
                   lbp_ref, wgk_ref, bgk_ref, anw_ref, bnw_ref, wua_ref, wub_ref, wo_ref,
                   ya_ref, yb_ref, wua_bf_ref, wub_bf_ref, wo_bf_ref, sta_ref, stb_ref):
    @pl.when(pl.program_id(1) == 0)
    def _():
        sta_ref[...] = jnp.zeros_like(sta_ref)
        stb_ref[...] = jnp.zeros_like(stb_ref)

    for src, dst in ((wua_ref, wua_bf_ref), (wub_ref, wub_bf_ref), (wo_ref, wo_bf_ref)):
        dst[...] = src[...].astype(dst.dtype)

    lbp = lbp_ref[...]
    pe = jnp.exp(lbp - jnp.max(lbp, axis=0, keepdims=True))
    lb = pe[0:1, :] / jnp.sum(pe, axis=0, keepdims=True)
    anw, bnw = anw_ref[...], bnw_ref[...]
    a_cols = [slice(h * A_HEAD_DIM, (h + 1) * A_HEAD_DIM) for h in range(A_HEADS)]
    b_cols = [slice(h * B_VAL_DIM, (h + 1) * B_VAL_DIM) for h in range(B_HEADS)]
    st_refs = [sta_ref.at[h] for h in range(A_HEADS)] + [stb_ref.at[h] for h in range(B_HEADS)]

    for ci in range(CHUNKS_PER_STEP):
        rows = slice(ci * CHUNK, (ci + 1) * CHUNK)
        qa = qa_ref[rows, :]
        f = lb + (1.0 - lb) * _sigmoid(fa_ref[rows, :])
        z2 = (_dot(gkl_ref[rows, :].astype(BF16), wgk_ref[...]) + bgk_ref[...]) * LOG2E
        gk = (jnp.minimum(z2, 0.0) - jnp.log2(1.0 + jnp.exp2(-jnp.abs(z2)))) * (1.0 / GATE_LOGIT_NORMALIZER)

        q = jnp.concatenate([qa * _sigmoid(qa), qb_ref[rows, :] * (B_KEY_DIM ** -0.5)], axis=1)
        k = jnp.concatenate([1.0 - f, kb_ref[rows, :]], axis=1)
        g2 = jnp.concatenate([jnp.log2(f), gk], axis=1)
        v_heads = [ia_ref[rows, sl] for sl in a_cols] + [vb_ref[rows, sl] for sl in b_cols]
        outs = _chunk_attend(q, k, g2, v_heads, st_refs)

        for sl, o in zip(a_cols, outs[:A_HEADS]):
            ya_ref[rows, sl] = _gated_head_norm(o, ga_ref[rows, sl], anw).astype(ya_ref.dtype)
        for sl, o in zip(b_cols, outs[A_HEADS:]):
            yb_ref[rows, sl] = _gated_head_norm(o, gb_ref[rows, sl], bnw).astype(yb_ref.dtype)


def _mixers(proj, gk_low, lb_param, w_gk2p, b_gk2, a_norm_w, b_norm_w, merge_weights, bsz, seq):
    m = proj.shape[0]
    tm = CHUNK * CHUNKS_PER_STEP
    nc = seq // tm
    steps = bsz * nc
    assert A_HEAD_DIM == B_KEY_DIM
    assert all(w.shape[0] % (steps * 16) == 0 for w in merge_weights)

    def share(w):
        return pl.BlockSpec((w.shape[0] // steps, w.shape[1]), lambda b, c: (b * nc + c, 0))

    def col(c0, width):
        assert c0 % width == 0
        return pl.BlockSpec((tm, width), lambda b, c: (b * nc + c, c0 // width))

    const = lambda b, c: (0, 0)
    return pl.pallas_call(
        _mixers_kernel,
        grid=(bsz, nc),
        in_specs=[
            col(COL_QA, A_WIDTH), col(COL_FA, A_WIDTH), col(COL_IA, A_WIDTH), col(COL_GA, A_WIDTH),
            col(COL_QB, B_KEY_WIDTH), col(COL_KB, B_KEY_WIDTH),
            col(COL_VB, B_VAL_WIDTH), col(COL_GB, B_VAL_WIDTH),
            col(0, LANES),
            pl.BlockSpec(lb_param.shape, const),
            pl.BlockSpec((LANES, B_KEY_WIDTH), const),
            pl.BlockSpec((1, B_KEY_WIDTH), const),
            pl.BlockSpec((1, A_HEAD_DIM), const),
            pl.BlockSpec((1, B_VAL_DIM), const),
        ] + [share(w) for w in merge_weights],
        out_specs=[col(0, A_WIDTH), col(0, B_VAL_WIDTH)] + [share(w) for w in merge_weights],
        out_shape=[
            jax.ShapeDtypeStruct((m, A_WIDTH), BF16),
            jax.ShapeDtypeStruct((m, B_VAL_WIDTH), BF16),
        ] + [jax.ShapeDtypeStruct(w.shape, BF16) for w in merge_weights],
        scratch_shapes=[
            pltpu.VMEM((A_HEADS, A_HEAD_DIM, A_HEAD_DIM), F32),
            pltpu.VMEM((B_HEADS, B_VAL_DIM, B_KEY_DIM), F32),
        ],
        compiler_params=pltpu.CompilerParams(
            dimension_semantics=("arbitrary", "arbitrary"), vmem_limit_bytes=VMEM_LIMIT),
        name="mixers",
    )(proj, proj, proj, proj, proj, proj, proj, proj, gk_low, lb_param, w_gk2p, b_gk2, a_norm_w, b_norm_w,
      *merge_weights)


def _merge_kernel(ya_ref, yb_ref, ma_ref, mb_ref, x_ref, mod_ref, nw_ref,
                  wua_ref, wub_ref, wo_ref, x1_ref, h2_ref):
    ua = _dot(ya_ref[...], wua_ref[...])
    ub = _dot(yb_ref[...], wub_ref[...])
    merged = _sigmoid(ma_ref[...]) * ua + _sigmoid(mb_ref[...]) * ub
    out = _dot(merged.astype(BF16), wo_ref[...])
    x1 = x_ref[...] + mod_ref[0, 2:3, :] * out
    x1_ref[...] = x1
    scale = nw_ref[...] * (1.0 + mod_ref[0, 4:5, :])
    h2_ref[...] = (_rms_rows(x1) * scale + mod_ref[0, 3:4, :]).astype(h2_ref.dtype)


def _merge(ya, yb, proj, x2, mod, norm_w, w_up_a, w_up_b, w_o, seq):
    m, d = x2.shape
    assert COL_MA % d == 0 and COL_MB % d == 0
    tiles_per_seq = seq // TM_MERGE
    row = lambda i: (i, 0)
    const = lambda i: (0, 0)
    return pl.pallas_call(
        _merge_kernel,
        grid=(m // TM_MERGE,),
        in_specs=[
            pl.BlockSpec((TM_MERGE, A_WIDTH), row),
            pl.BlockSpec((TM_MERGE, B_VAL_WIDTH), row),
            pl.BlockSpec((TM_MERGE, d), lambda i: (i, COL_MA // d)),
            pl.BlockSpec((TM_MERGE, d), lambda i: (i, COL_MB // d)),
            pl.BlockSpec((TM_MERGE, d), row),
            pl.BlockSpec((1, N_MOD, d), lambda i: (i // tiles_per_seq, 0, 0)),
            pl.BlockSpec((1, d), const),
            pl.BlockSpec(w_up_a.shape, const),
            pl.BlockSpec(w_up_b.shape, const),
            pl.BlockSpec(w_o.shape, const),
        ],
        out_specs=[pl.BlockSpec((TM_MERGE, d), row), pl.BlockSpec((TM_MERGE, d), row)],
        out_shape=[jax.ShapeDtypeStruct((m, d), F32), jax.ShapeDtypeStruct((m, d), BF16)],
        compiler_params=pltpu.CompilerParams(
            dimension_semantics=("arbitrary",), vmem_limit_bytes=VMEM_LIMIT),
        name="merge",
    )(ya, yb, proj, proj, x2, mod, norm_w, w_up_a, w_up_b, w_o)


def _ffn_up_kernel(h_ref, wg_ref, wu_ref, wdown_ref, o_ref, wdown_bf_ref):
    for r in range(0, o_ref.shape[0], FFN_UP_SUBROWS):
        rs = slice(r, r + FFN_UP_SUBROWS)
        h = h_ref[rs, :]
        for t in range(0, o_ref.shape[1], FFN_UP_SUBTILE):
            sl = slice(t, t + FFN_UP_SUBTILE)
            gate = _dot(h, wg_ref[:, sl].astype(BF16))
            up = _dot(h, wu_ref[:, sl].astype(BF16))
            o_ref[rs, sl] = (gate * _sigmoid(gate) * up).astype(o_ref.dtype)
    wdown_bf_ref[...] = wdown_ref[...].astype(wdown_bf_ref.dtype)


def _ffn_up(h2, w_in, w_down):
    m, d = h2.shape
    hidden = w_in.shape[1] // 2
    nf = hidden // TF_FFN
    steps = (m // TM_FFN_UP) * nf
    assert w_down.shape == (hidden, d) and hidden % (steps * 16) == 0
    share = hidden // steps
    return pl.pallas_call(
        _ffn_up_kernel,
        grid=(m // TM_FFN_UP, nf),
        in_specs=[
            pl.BlockSpec((TM_FFN_UP, d), lambda i, j: (i, 0)),
            pl.BlockSpec((d, TF_FFN), lambda i, j: (0, j)),
            pl.BlockSpec((d, TF_FFN), lambda i, j: (0, j + nf)),
            pl.BlockSpec((share, d), lambda i, j: (i * nf + j, 0)),
        ],
        out_specs=[
            pl.BlockSpec((TM_FFN_UP, TF_FFN), lambda i, j: (i, j)),
            pl.BlockSpec((share, d), lambda i, j: (i * nf + j, 0)),
        ],
        out_shape=[
            jax.ShapeDtypeStruct((m, hidden), BF16),
            jax.ShapeDtypeStruct((hidden, d), BF16),
        ],
        compiler_params=pltpu.CompilerParams(
            dimension_semantics=("arbitrary", "arbitrary"), vmem_limit_bytes=VMEM_LIMIT),
        name="ffn_up",
    )(h2, w_in, w_in, w_down)


def _ffn_down_kernel(a_ref, wo_ref, x1_ref, mod_ref, nw_ref, o_ref):
    x2 = x1_ref[...] + mod_ref[0, 5:6, :] * _dot(a_ref[...], wo_ref[...])
    o_ref[...] = _rms_rows(x2) * nw_ref[...]


def _ffn_down(act, w_out, x1, mod, norm_w, seq):
    m, d = x1.shape
    hidden = w_out.shape[0]
    tiles_per_seq = seq // TM_FFN_DOWN
    return pl.pallas_call(
        _ffn_down_kernel,
        grid=(m // TM_FFN_DOWN,),
        in_specs=[
            pl.BlockSpec((TM_FFN_DOWN, hidden), lambda i: (i, 0)),
            pl.BlockSpec((hidden, d), lambda i: (0, 0)),
            pl.BlockSpec((TM_FFN_DOWN, d), lambda i: (i, 0)),
            pl.BlockSpec((1, N_MOD, d), lambda i: (i // tiles_per_seq, 0, 0)),
            pl.BlockSpec((1, d), lambda i: (0, 0)),
        ],
        out_specs=pl.BlockSpec((TM_FFN_DOWN, d), lambda i: (i, 0)),
        out_shape=jax.ShapeDtypeStruct((m, d), F32),
        compiler_params=pltpu.CompilerParams(
            dimension_semantics=("arbitrary",), vmem_limit_bytes=VMEM_LIMIT),
        name="ffn_down",
    )(act, w_out, x1, mod, norm_w)


def kernel(x, c, w_ada, b_ada, norm1_w, w_in, w_gk2, b_gk2, lb_param, a_norm_w, b_norm_w,
           w_up_a, w_up_b, w_o, norm2_w, w_ffn_in, w_ffn_out, final_norm_w):
    bsz, seq, d = x.shape
    depth = w_in.shape[0]
    assert depth == 1 and d == D_MODEL and w_in.shape[2] == MAIN_WIDTH + GK_RANK
    assert seq % TM_PROJ == 0 and seq % (CHUNK * CHUNKS_PER_STEP) == 0 and seq % TM_MERGE == 0
    assert seq % TM_FFN_UP == 0 and seq % TM_FFN_DOWN == 0
    m = bsz * seq
    x2 = x.reshape(m, d)

    def layer0(w):
        return w.reshape(w.shape[1:])

    mod = _ada(c, layer0(w_ada), b_ada).reshape(bsz, N_MOD, d)

    w_in_t = jnp.swapaxes(layer0(w_in), 0, 1).astype(BF16)
    proj, gk_low = _inproj(x2, mod, norm1_w, w_in_t, seq)

    w_gk2p = jnp.pad(layer0(w_gk2).astype(BF16), ((0, LANES - GK_RANK), (0, 0)))
    ya, yb, wua, wub, wo = _mixers(proj, gk_low, lb_param, w_gk2p, b_gk2, a_norm_w, b_norm_w,
                                   (layer0(w_up_a), layer0(w_up_b), layer0(w_o)), bsz, seq)

    x1, h2 = _merge(ya, yb, proj, x2, mod, norm2_w, wua, wub, wo, seq)

    act, w_down = _ffn_up(h2, layer0(w_ffn_in), layer0(w_ffn_out))
    out = _ffn_down(act, w_down, x1, mod, final_norm_w.reshape(1, d), seq)
    return out.reshape(bsz, seq, d)
```

```python
import functools
import math

import jax
import jax.numpy as jnp
from jax import lax
from jax.experimental import pallas as pl
from jax.experimental.pallas import tpu as pltpu

F32 = jnp.float32
BF16 = jnp.bfloat16

D_MODEL = 2048
A_HEADS = 8
A_HEAD_DIM = 128
A_WIDTH = A_HEADS * A_HEAD_DIM
B_HEADS = 4
B_KEY_DIM = 128
B_VAL_DIM = 256
B_KEY_WIDTH = B_HEADS * B_KEY_DIM
B_VAL_WIDTH = B_HEADS * B_VAL_DIM
GK_RANK = 16
GATE_LOGIT_NORMALIZER = 16.0
N_MOD = 6
EPS = 1e-6
LOG2E = math.log2(math.e)

LANES = 128
SUBLANES = 8
VMEM_LIMIT = 56 * 1024 * 1024

MAIN_WIDTH = 4 * A_WIDTH + 2 * B_KEY_WIDTH + 2 * B_VAL_WIDTH + 2 * D_MODEL
COL_MA, COL_MB = 0, D_MODEL
COL_QA = 2 * D_MODEL
COL_FA, COL_IA, COL_GA = COL_QA + A_WIDTH, COL_QA + 2 * A_WIDTH, COL_QA + 3 * A_WIDTH
COL_QB = COL_QA + 4 * A_WIDTH
COL_KB = COL_QB + B_KEY_WIDTH
COL_VB = COL_KB + B_KEY_WIDTH
COL_GB = COL_VB + B_VAL_WIDTH

CHUNK = 128
CHUNKS_PER_STEP = 2
TM_PROJ = 1024
TN_PROJ = 1024
TM_MERGE = 256
TM_FFN_UP = 2048
TF_FFN = 512
FFN_UP_SUBTILE = 256
FFN_UP_SUBROWS = 1024
TM_FFN_DOWN = 512
TN_ADA = 2048


def _sigmoid(x):
    return 1.0 / (1.0 + jnp.exp2(x * (-LOG2E)))


def _rms_rows(x):
    return x * lax.rsqrt(jnp.mean(x * x, axis=-1, keepdims=True) + EPS)


def _dot(a, b):
    return jnp.dot(a, b, preferred_element_type=F32)


def _dot_nt(a, b):
    return lax.dot_general(a, b, (((1,), (1,)), ((), ())), preferred_element_type=F32)


def _dot_tn(a, b):
    return lax.dot_general(a, b, (((0,), (0,)), ((), ())), preferred_element_type=F32)


def _ada_kernel(c_ref, w_ref, b_ref, o_ref):
    c = c_ref[...]
    cond = (c * _sigmoid(c)).astype(BF16)
    o_ref[...] = _dot(cond, w_ref[...].astype(BF16)) + b_ref[...]


def _ada(c, w, b):
    bsz, d = c.shape
    n = w.shape[1]
    return pl.pallas_call(
        _ada_kernel,
        grid=(n // TN_ADA,),
        in_specs=[
            pl.BlockSpec((bsz, d), lambda j: (0, 0)),
            pl.BlockSpec((d, TN_ADA), lambda j: (0, j)),
            pl.BlockSpec((1, TN_ADA), lambda j: (0, j)),
        ],
        out_specs=pl.BlockSpec((bsz, TN_ADA), lambda j: (0, j)),
        out_shape=jax.ShapeDtypeStruct((bsz, n), F32),
        compiler_params=pltpu.CompilerParams(
            dimension_semantics=("arbitrary",), vmem_limit_bytes=VMEM_LIMIT),
        name="ada",
    )(c, w, b)


def _inproj_kernel(x_ref, mod_ref, nw_ref, w_ref, wgk_ref, o_ref, gk_ref, h_scr):
    @pl.when(pl.program_id(1) == 0)
    def _():
        scale = nw_ref[...] * (1.0 + mod_ref[0, 1:2, :])
        hb = (_rms_rows(x_ref[...]) * scale + mod_ref[0, 0:1, :]).astype(BF16)
        h_scr[...] = hb
        gk_ref[...] = _dot_nt(hb, wgk_ref[...].astype(BF16))

    o_ref[...] = _dot_nt(h_scr[...], w_ref[...].astype(BF16))


def _inproj(x2, mod, norm_w, w_t, seq):
    m, d = x2.shape
    n_mix = MAIN_WIDTH - 2 * d
    gate0 = n_mix + GK_RANK
    assert w_t.shape == (MAIN_WIDTH + GK_RANK, d) and n_mix % TN_PROJ == 0 and (2 * d) % TN_PROJ == 0
    assert gate0 % 16 == 0
    ng = 2 * d // TN_PROJ
    tiles_per_seq = seq // TM_PROJ

    def w_rows(i, j):
        return (pl.multiple_of(jnp.where(j < ng, gate0 + j * TN_PROJ, (j - ng) * TN_PROJ), 16), 0)

    return pl.pallas_call(
        _inproj_kernel,
        grid=(m // TM_PROJ, MAIN_WIDTH // TN_PROJ),
        in_specs=[
            pl.BlockSpec((TM_PROJ, d), lambda i, j: (i, 0)),
            pl.BlockSpec((1, N_MOD, d), lambda i, j: (i // tiles_per_seq, 0, 0)),
            pl.BlockSpec((1, d), lambda i, j: (0, 0)),
            pl.BlockSpec((pl.Element(TN_PROJ), pl.Element(d)), w_rows),
            pl.BlockSpec((pl.Element(LANES), pl.Element(d)), lambda i, j: (n_mix, 0)),
        ],
        out_specs=[
            pl.BlockSpec((TM_PROJ, TN_PROJ), lambda i, j: (i, j)),
            pl.BlockSpec((TM_PROJ, LANES), lambda i, j: (i, 0)),
        ],
        out_shape=[
            jax.ShapeDtypeStruct((m, MAIN_WIDTH), F32),
            jax.ShapeDtypeStruct((m, LANES), F32),
        ],
        scratch_shapes=[pltpu.VMEM((TM_PROJ, d), BF16)],
        compiler_params=pltpu.CompilerParams(
            dimension_semantics=("arbitrary", "arbitrary"), vmem_limit_bytes=VMEM_LIMIT),
        name="inproj",
    )(x2, mod, norm_w, w_t, w_t)


class _ScoreMasks:
    def __init__(self, c):
        sub = lax.broadcasted_iota(jnp.int32, (SUBLANES, c), 0)
        self.lane = lax.broadcasted_iota(jnp.int32, (SUBLANES, c), 1)
        in_tile = self.lane & (SUBLANES - 1)
        self.low_bit = sub ^ in_tile
        self.diag = jnp.where(in_tile < sub, self.lane >> int(math.log2(SUBLANES)), -1)

    def level(self, g):
        return (self.low_bit & -g) == g


def _group_mid_rows(p, g):
    c, dk = p.shape
    p3 = p.reshape(c // SUBLANES, SUBLANES, dk)
    sub = lax.broadcasted_iota(jnp.int32, p3.shape, 1)
    out = None
    for s in range(0, SUBLANES, 2 * g):
        part = jnp.broadcast_to(p3[:, s + g - 1:s + g, :], p3.shape)
        out = part if out is None else jnp.where(sub >= s, part, out)
    return out.reshape(c, dk)


def _chunk_attend(q, k, g2, v_heads, st_refs):
    c, width = q.shape
    heads = len(v_heads)
    dk = width // heads
    groups = c // SUBLANES
    hk = lambda a, h: a[:, h * dk:(h + 1) * dk]
    rows = lambda a, r: a[r * SUBLANES:(r + 1) * SUBLANES, :]
    row = lax.broadcasted_iota(jnp.int32, (c, width), 0)
    masks = _ScoreMasks(c)

    p = g2
    small = [[None] * groups for _ in range(heads)]
    g = c // 2
    plan = []
    while g >= SUBLANES:
        plan.append(g)
        g //= 2

    g = 1
    big_scores = {}
    while g < c:
        if g < SUBLANES:
            second = (row & g) != 0
            if g == 1:
                t = pltpu.roll(p, 1, axis=0)
                expo = jnp.where(second, p, 0.0)
            else:
                t = _group_mid_rows(p, g)
                expo = jnp.where(second, p, t - p)
            z = jnp.where(second, q, k) * jnp.exp2(expo)
            p = p + jnp.where(second, t, 0.0)
            zb = z.astype(BF16)
            lvl = None if g == 1 else masks.level(g)
            for h in range(heads):
                s = _dot_nt(hk(zb, h), hk(zb, h))
                for r in range(groups):
                    small[h][r] = rows(s, r) if g == 1 else jnp.where(lvl, rows(s, r), small[h][r])
        else:
            zs, ps = [], []
            for s0 in range(0, c, 2 * g):
                p1, p2 = p[s0:s0 + g, :], p[s0 + g:s0 + 2 * g, :]
                t = jnp.broadcast_to(p1[g - 1:g, :], (g, width))
                zs += [k[s0:s0 + g, :] * jnp.exp2(t - p1), q[s0 + g:s0 + 2 * g, :] * jnp.exp2(p2)]
                ps += [p1, p2 + t]
            zb = jnp.concatenate(zs, axis=0).astype(BF16)
            p = jnp.concatenate(ps, axis=0)
            zq = jnp.concatenate([zb[s0 + g:s0 + 2 * g, :] for s0 in range(0, c, 2 * g)], axis=0)
            big_scores[g] = [_dot_nt(hk(zq, h), hk(zb, h)) for h in range(heads)]
        g *= 2


    att = []
    for h in range(heads):
        out_rows = []
        for r in range(groups):
            first_row = r * SUBLANES
            acc, edge = None, 0
            for g in plan:
                if first_row & g:
                    blk = (first_row // (2 * g)) * g + (first_row % g)
                    piece = big_scores[g][h][blk:blk + SUBLANES, :]
                    acc = piece if acc is None else jnp.where(masks.lane < edge, acc, piece)
                    edge += g
            if acc is None:
                acc = jnp.zeros((SUBLANES, c), F32)
            else:
                acc = jnp.where(masks.lane < edge, acc, 0.0)
            out_rows.append(jnp.where(masks.diag == r, small[h][r], acc))
        att.append(jnp.concatenate(out_rows, axis=0).astype(BF16))

    qk = q * k
    q_dec = (q * jnp.exp2(p)).astype(BF16)
    p_last = p[c - 1:c, :]
    k_dec = (k * jnp.exp2(p_last - p)).astype(BF16)
    st_scale = jnp.exp2(p_last)
    outs = []
    for h in range(heads):
        v = v_heads[h]
        vb = v.astype(BF16)
        st = st_refs[h][...]
        o = _dot(att[h], vb)
        o = o + jnp.sum(hk(qk, h), axis=-1, keepdims=True) * v
        o = o + _dot_nt(hk(q_dec, h), st.astype(BF16))
        st_refs[h][...] = st * hk(st_scale, h) + _dot_tn(vb, hk(k_dec, h))
        outs.append(o)
    return outs


def _gated_head_norm(o, gate, w):
    return _rms_rows(o) * w * (gate * _sigmoid(gate))


def _mixers_kernel(qa_ref, fa_ref, ia_ref, ga_ref, qb_ref, kb_ref, vb_ref, gb_ref, gkl_ref,
                   lbp_ref, wgk_ref, bgk_ref, anw_ref, bnw_ref, wua_ref, wub_ref, wo_ref,
                   ya_ref, yb_ref, wua_bf_ref, wub_bf_ref, wo_bf_ref, sta_ref, stb_ref):
    @pl.when(pl.program_id(1) == 0)
    def _():
        sta_ref[...] = jnp.zeros_like(sta_ref)
        stb_ref[...] = jnp.zeros_like(stb_ref)

    for src, dst in ((wua_ref, wua_bf_ref), (wub_ref, wub_bf_ref), (wo_ref, wo_bf_ref)):
        dst[...] = src[...].astype(dst.dtype)

    lbp = lbp_ref[...]
    pe = jnp.exp(lbp - jnp.max(lbp, axis=0, keepdims=True))
    lb = pe[0:1, :] / jnp.sum(pe, axis=0, keepdims=True)
    anw, bnw = anw_ref[...], bnw_ref[...]
    a_cols = [slice(h * A_HEAD_DIM, (h + 1) * A_HEAD_DIM) for h in range(A_HEADS)]
    b_cols = [slice(h * B_VAL_DIM, (h + 1) * B_VAL_DIM) for h in range(B_HEADS)]
    st_refs = [sta_ref.at[h] for h in range(A_HEADS)] + [stb_ref.at[h] for h in range(B_HEADS)]

    for ci in range(CHUNKS_PER_STEP):
        rows = slice(ci * CHUNK, (ci + 1) * CHUNK)
        qa = qa_ref[rows, :]
        f = lb + (1.0 - lb) * _sigmoid(fa_ref[rows, :])
        z2 = (_dot(gkl_ref[rows, :].astype(BF16), wgk_ref[...]) + bgk_ref[...]) * LOG2E
        gk = (jnp.minimum(z2, 0.0) - jnp.log2(1.0 + jnp.exp2(-jnp.abs(z2)))) * (1.0 / GATE_LOGIT_NORMALIZER)

        q = jnp.concatenate([qa * _sigmoid(qa), qb_ref[rows, :] * (B_KEY_DIM ** -0.5)], axis=1)
        k = jnp.concatenate([1.0 - f, kb_ref[rows, :]], axis=1)
        g2 = jnp.concatenate([jnp.log2(f), gk], axis=1)
        v_heads = [ia_ref[rows, sl] for sl in a_cols] + [vb_ref[rows, sl] for sl in b_cols]
        outs = _chunk_attend(q, k, g2, v_heads, st_refs)

        for sl, o in zip(a_cols, outs[:A_HEADS]):
            ya_ref[rows, sl] = _gated_head_norm(o, ga_ref[rows, sl], anw).astype(ya_ref.dtype)
        for sl, o in zip(b_cols, outs[A_HEADS:]):
            yb_ref[rows, sl] = _gated_head_norm(o, gb_ref[rows, sl], bnw).astype(yb_ref.dtype)


def _mixers(proj, gk_low, lb_param, w_gk2p, b_gk2, a_norm_w, b_norm_w, merge_weights, bsz, seq):
    m = proj.shape[0]
    tm = CHUNK * CHUNKS_PER_STEP
    nc = seq // tm
    steps = bsz * nc
    assert A_HEAD_DIM == B_KEY_DIM
    assert all(w.shape[0] % (steps * 16) == 0 for w in merge_weights)

    def share(w):
        return pl.BlockSpec((w.shape[0] // steps, w.shape[1]), lambda b, c: (b * nc + c, 0))

    def col(c0, width):
        assert c0 % width == 0
        return pl.BlockSpec((tm, width), lambda b, c: (b * nc + c, c0 // width))

    const = lambda b, c: (0, 0)
    return pl.pallas_call(
        _mixers_kernel,
        grid=(bsz, nc),
        in_specs=[
            col(COL_QA, A_WIDTH), col(COL_FA, A_WIDTH), col(COL_IA, A_WIDTH), col(COL_GA, A_WIDTH),
            col(COL_QB, B_KEY_WIDTH), col(COL_KB, B_KEY_WIDTH),
            col(COL_VB, B_VAL_WIDTH), col(COL_GB, B_VAL_WIDTH),
            col(0, LANES),
            pl.BlockSpec(lb_param.shape, const),
            pl.BlockSpec((LANES, B_KEY_WIDTH), const),
            pl.BlockSpec((1, B_KEY_WIDTH), const),
            pl.BlockSpec((1, A_HEAD_DIM), const),
            pl.BlockSpec((1, B_VAL_DIM), const),
        ] + [share(w) for w in merge_weights],
        out_specs=[col(0, A_WIDTH), col(0, B_VAL_WIDTH)] + [share(w) for w in merge_weights],
        out_shape=[
            jax.ShapeDtypeStruct((m, A_WIDTH), BF16),
            jax.ShapeDtypeStruct((m, B_VAL_WIDTH), BF16),
        ] + [jax.ShapeDtypeStruct(w.shape, BF16) for w in merge_weights],
        scratch_shapes=[
            pltpu.VMEM((A_HEADS, A_HEAD_DIM, A_HEAD_DIM), F32),
            pltpu.VMEM((B_HEADS, B_VAL_DIM, B_KEY_DIM), F32),
        ],
        compiler_params=pltpu.CompilerParams(
            dimension_semantics=("arbitrary", "arbitrary"), vmem_limit_bytes=VMEM_LIMIT),
        name="mixers",
    )(proj, proj, proj, proj, proj, proj, proj, proj, gk_low, lb_param, w_gk2p, b_gk2, a_norm_w, b_norm_w,
      *merge_weights)


def _merge_kernel(ya_ref, yb_ref, ma_ref, mb_ref, x_ref, mod_ref, nw_ref,
                  wua_ref, wub_ref, wo_ref, x1_ref, h2_ref):
    ua = _dot(ya_ref[...], wua_ref[...])
    ub = _dot(yb_ref[...], wub_ref[...])
    merged = _sigmoid(ma_ref[...]) * ua + _sigmoid(mb_ref[...]) * ub
    out = _dot(merged.astype(BF16), wo_ref[...])
    x1 = x_ref[...] + mod_ref[0, 2:3, :] * out
    x1_ref[...] = x1
    scale = nw_ref[...] * (1.0 + mod_ref[0, 4:5, :])
    h2_ref[...] = (_rms_rows(x1) * scale + mod_ref[0, 3:4, :]).astype(h2_ref.dtype)


def _merge(ya, yb, proj, x2, mod, norm_w, w_up_a, w_up_b, w_o, seq):
    m, d = x2.shape
    assert COL_MA % d == 0 and COL_MB % d == 0
    tiles_per_seq = seq // TM_MERGE
    row = lambda i: (i, 0)
    const = lambda i: (0, 0)
    return pl.pallas_call(
        _merge_kernel,
        grid=(m // TM_MERGE,),
        in_specs=[
            pl.BlockSpec((TM_MERGE, A_WIDTH), row),
            pl.BlockSpec((TM_MERGE, B_VAL_WIDTH), row),
            pl.BlockSpec((TM_MERGE, d), lambda i: (i, COL_MA // d)),
            pl.BlockSpec((TM_MERGE, d), lambda i: (i, COL_MB // d)),
            pl.BlockSpec((TM_MERGE, d), row),
            pl.BlockSpec((1, N_MOD, d), lambda i: (i // tiles_per_seq, 0, 0)),
            pl.BlockSpec((1, d), const),
            pl.BlockSpec(w_up_a.shape, const),
            pl.BlockSpec(w_up_b.shape, const),
            pl.BlockSpec(w_o.shape, const),
        ],
        out_specs=[pl.BlockSpec((TM_MERGE, d), row), pl.BlockSpec((TM_MERGE, d), row)],
        out_shape=[jax.ShapeDtypeStruct((m, d), F32), jax.ShapeDtypeStruct((m, d), BF16)],
        compiler_params=pltpu.CompilerParams(
            dimension_semantics=("arbitrary",), vmem_limit_bytes=VMEM_LIMIT),
        name="merge",
    )(ya, yb, proj, proj, x2, mod, norm_w, w_up_a, w_up_b, w_o)


def _ffn_up_kernel(h_ref, wg_ref, wu_ref, wdown_ref, o_ref, wdown_bf_ref):
    for r in range(0, o_ref.shape[0], FFN_UP_SUBROWS):
        rs = slice(r, r + FFN_UP_SUBROWS)
        h = h_ref[rs, :]
        for t in range(0, o_ref.shape[1], FFN_UP_SUBTILE):
            sl = slice(t, t + FFN_UP_SUBTILE)
            gate = _dot(h, wg_ref[:, sl].astype(BF16))
            up = _dot(h, wu_ref[:, sl].astype(BF16))
            o_ref[rs, sl] = (gate * _sigmoid(gate) * up).astype(o_ref.dtype)
    wdown_bf_ref[...] = wdown_ref[...].astype(wdown_bf_ref.dtype)


def _ffn_up(h2, w_in, w_down):
    m, d = h2.shape
    hidden = w_in.shape[1] // 2
    nf = hidden // TF_FFN
    steps = (m // TM_FFN_UP) * nf
    assert w_down.shape == (hidden, d) and hidden % (steps * 16) == 0
    share = hidden // steps
    return pl.pallas_call(
        _ffn_up_kernel,
        grid=(m // TM_FFN_UP, nf),
        in_specs=[
            pl.BlockSpec((TM_FFN_UP, d), lambda i, j: (i, 0)),
            pl.BlockSpec((d, TF_FFN), lambda i, j: (0, j)),
            pl.BlockSpec((d, TF_FFN), lambda i, j: (0, j + nf)),
            pl.BlockSpec((share, d), lambda i, j: (i * nf + j, 0)),
        ],
        out_specs=[
            pl.BlockSpec((TM_FFN_UP, TF_FFN), lambda i, j: (i, j)),
            pl.BlockSpec((share, d), lambda i, j: (i * nf + j, 0)),
        ],
        out_shape=[
            jax.ShapeDtypeStruct((m, hidden), BF16),
            jax.ShapeDtypeStruct((hidden, d), BF16),
        ],
        compiler_params=pltpu.CompilerParams(
            dimension_semantics=("arbitrary", "arbitrary"), vmem_limit_bytes=VMEM_LIMIT),
        name="ffn_up",
    )(h2, w_in, w_in, w_down)


def _ffn_down_kernel(a_ref, wo_ref, x1_ref, mod_ref, nw_ref, o_ref):
    x2 = x1_ref[...] + mod_ref[0, 5:6, :] * _dot(a_ref[...], wo_ref[...])
    o_ref[...] = _rms_rows(x2) * nw_ref[...]


def _ffn_down(act, w_out, x1, mod, norm_w, seq):
    m, d = x1.shape
    hidden = w_out.shape[0]
    tiles_per_seq = seq // TM_FFN_DOWN
    return pl.pallas_call(
        _ffn_down_kernel,
        grid=(m // TM_FFN_DOWN,),
        in_specs=[
            pl.BlockSpec((TM_FFN_DOWN, hidden), lambda i: (i, 0)),
            pl.BlockSpec((hidden, d), lambda i: (0, 0)),
            pl.BlockSpec((TM_FFN_DOWN, d), lambda i: (i, 0)),
            pl.BlockSpec((1, N_MOD, d), lambda i: (i // tiles_per_seq, 0, 0)),
            pl.BlockSpec((1, d), lambda i: (0, 0)),
        ],
        out_specs=pl.BlockSpec((TM_FFN_DOWN, d), lambda i: (i, 0)),
        out_shape=jax.ShapeDtypeStruct((m, d), F32),
        compiler_params=pltpu.CompilerParams(
            dimension_semantics=("arbitrary",), vmem_limit_bytes=VMEM_LIMIT),
        name="ffn_down",
    )(act, w_out, x1, mod, norm_w)


def kernel(x, c, w_ada, b_ada, norm1_w, w_in, w_gk2, b_gk2, lb_param, a_norm_w, b_norm_w,
           w_up_a, w_up_b, w_o, norm2_w, w_ffn_in, w_ffn_out, final_norm_w):
    bsz, seq, d = x.shape
    depth = w_in.shape[0]
    assert depth == 1 and d == D_MODEL and w_in.shape[2] == MAIN_WIDTH + GK_RANK
    assert seq % TM_PROJ == 0 and seq % (CHUNK * CHUNKS_PER_STEP) == 0 and seq % TM_MERGE == 0
    assert seq % TM_FFN_UP == 0 and seq % TM_FFN_DOWN == 0
    m = bsz * seq
    x2 = x.reshape(m, d)

    def layer0(w):
        return w.reshape(w.shape[1:])

    mod = _ada(c, layer0(w_ada), b_ada).reshape(bsz, N_MOD, d)

    w_in_t = jnp.swapaxes(layer0(w_in), 0, 1)
    proj, gk_low = _inproj(x2, mod, norm1_w, w_in_t, seq)

    w_gk2p = jnp.pad(layer0(w_gk2).astype(BF16), ((0, LANES - GK_RANK), (0, 0)))
    ya, yb, wua, wub, wo = _mixers(proj, gk_low, lb_param, w_gk2p, b_gk2, a_norm_w, b_norm_w,
                                   (layer0(w_up_a), layer0(w_up_b), layer0(w_o)), bsz, seq)

    x1, h2 = _merge(ya, yb, proj, x2, mod, norm2_w, wua, wub, wo, seq)

    act, w_down = _ffn_up(h2, layer0(w_ffn_in), layer0(w_ffn_out))
    out = _ffn_down(act, w_down, x1, mod, final_norm_w.reshape(1, d), seq)
    return out.reshape(bsz, seq, d)
```

```python
import functools
import math

import jax
import jax.numpy as jnp
from jax import lax
from jax.experimental import pallas as pl
from jax.experimental.pallas import tpu as pltpu

F32 = jnp.float32
BF16 = jnp.bfloat16

D_MODEL = 2048
A_HEADS = 8
A_HEAD_DIM = 128
A_WIDTH = A_HEADS * A_HEAD_DIM
B_HEADS = 4
B_KEY_DIM = 128
B_VAL_DIM = 256
B_KEY_WIDTH = B_HEADS * B_KEY_DIM
B_VAL_WIDTH = B_HEADS * B_VAL_DIM
GK_RANK = 16
GATE_LOGIT_NORMALIZER = 16.0
N_MOD = 6
EPS = 1e-6
LOG2E = math.log2(math.e)

LANES = 128
SUBLANES = 8
VMEM_LIMIT = 56 * 1024 * 1024
VMEM_LIMIT_PROJ = 60 * 1024 * 1024

MAIN_WIDTH = 4 * A_WIDTH + 2 * B_KEY_WIDTH + 2 * B_VAL_WIDTH + 2 * D_MODEL
COL_MA, COL_MB = 0, D_MODEL
COL_QA = 2 * D_MODEL
COL_FA, COL_IA, COL_GA = COL_QA + A_WIDTH, COL_QA + 2 * A_WIDTH, COL_QA + 3 * A_WIDTH
COL_QB = COL_QA + 4 * A_WIDTH
COL_KB = COL_QB + B_KEY_WIDTH
COL_VB = COL_KB + B_KEY_WIDTH
COL_GB = COL_VB + B_VAL_WIDTH

CHUNK = 128
CHUNKS_PER_STEP = 2
TM_PROJ = 1024
TN_PROJ = 2048
TM_MERGE = 256
TM_FFN_UP = 2048
TF_FFN = 512
FFN_UP_SUBTILE = 256
FFN_UP_SUBROWS = 1024
TM_FFN_DOWN = 512
TN_ADA = 1024


def _sigmoid(x):
    return 1.0 / (1.0 + jnp.exp2(x * (-LOG2E)))


def _rms_rows(x):
    return x * lax.rsqrt(jnp.mean(x * x, axis=-1, keepdims=True) + EPS)


def _dot(a, b):
    return jnp.dot(a, b, preferred_element_type=F32)


def _dot_nt(a, b):
    return lax.dot_general(a, b, (((1,), (1,)), ((), ())), preferred_element_type=F32)


def _dot_tn(a, b):
    return lax.dot_general(a, b, (((0,), (0,)), ((), ())), preferred_element_type=F32)


def _ada_kernel(c_ref, w_ref, b_ref, o_ref):
    c = c_ref[...]
    cond = (c * _sigmoid(c)).astype(BF16)
    o_ref[...] = _dot(cond, w_ref[...].astype(BF16)) + b_ref[...]


def _ada(c, w, b):
    bsz, d = c.shape
    n = w.shape[1]
    return pl.pallas_call(
        _ada_kernel,
        grid=(n // TN_ADA,),
        in_specs=[
            pl.BlockSpec((bsz, d), lambda j: (0, 0)),
            pl.BlockSpec((d, TN_ADA), lambda j: (0, j)),
            pl.BlockSpec((1, TN_ADA), lambda j: (0, j)),
        ],
        out_specs=pl.BlockSpec((bsz, TN_ADA), lambda j: (0, j)),
        out_shape=jax.ShapeDtypeStruct((bsz, n), F32),
        compiler_params=pltpu.CompilerParams(
            dimension_semantics=("arbitrary",), vmem_limit_bytes=VMEM_LIMIT),
        name="ada",
    )(c, w, b)


def _inproj_kernel(last_cols, x_ref, mod_ref, nw_ref, w_ref, wgk_ref, o_ref, gk_ref, h_scr):
    j = pl.program_id(1)
    last = pl.num_programs(1) - 1

    @pl.when(j == 0)
    def _():
        scale = nw_ref[...] * (1.0 + mod_ref[0, 1:2, :])
        hb = (_rms_rows(x_ref[...]) * scale + mod_ref[0, 0:1, :]).astype(BF16)
        h_scr[...] = hb
        gk_ref[...] = _dot_nt(hb, wgk_ref[...])

    @pl.when(j < last)
    def _():
        o_ref[...] = _dot_nt(h_scr[...], w_ref[...])

    @pl.when(j == last)
    def _():
        o_ref[:, :last_cols] = _dot_nt(h_scr[...], w_ref[:last_cols, :])
        if last_cols < o_ref.shape[1]:
            o_ref[:, last_cols:] = jnp.zeros((o_ref.shape[0], o_ref.shape[1] - last_cols), o_ref.dtype)


def _inproj(x2, mod, norm_w, w_t, seq):
    m, d = x2.shape
    n_mix = MAIN_WIDTH - 2 * d
    gate0 = n_mix + GK_RANK
    assert w_t.shape == (MAIN_WIDTH + GK_RANK, d) and (2 * d) % TN_PROJ == 0
    assert gate0 % 16 == 0
    ng = 2 * d // TN_PROJ
    nm = pl.cdiv(n_mix, TN_PROJ)
    last_cols = n_mix - (nm - 1) * TN_PROJ
    assert (nm - 1) * TN_PROJ + TN_PROJ <= w_t.shape[0]
    tiles_per_seq = seq // TM_PROJ

    def w_rows(i, j):
        return (pl.multiple_of(jnp.where(j < ng, gate0 + j * TN_PROJ, (j - ng) * TN_PROJ), 16), 0)

    return pl.pallas_call(
        functools.partial(_inproj_kernel, last_cols),
        grid=(m // TM_PROJ, ng + nm),
        in_specs=[
            pl.BlockSpec((TM_PROJ, d), lambda i, j: (i, 0)),
            pl.BlockSpec((1, N_MOD, d), lambda i, j: (i // tiles_per_seq, 0, 0)),
            pl.BlockSpec((1, d), lambda i, j: (0, 0)),
            pl.BlockSpec((pl.Element(TN_PROJ), pl.Element(d)), w_rows),
            pl.BlockSpec((pl.Element(LANES), pl.Element(d)), lambda i, j: (n_mix, 0)),
        ],
        out_specs=[
            pl.BlockSpec((TM_PROJ, TN_PROJ), lambda i, j: (i, j)),
            pl.BlockSpec((TM_PROJ, LANES), lambda i, j: (i, 0)),
        ],
        out_shape=[
            jax.ShapeDtypeStruct((m, (ng + nm) * TN_PROJ), F32),
            jax.ShapeDtypeStruct((m, LANES), F32),
        ],
        scratch_shapes=[pltpu.VMEM((TM_PROJ, d), BF16)],
        compiler_params=pltpu.CompilerParams(
            dimension_semantics=("arbitrary", "arbitrary"), vmem_limit_bytes=VMEM_LIMIT_PROJ),
        name="inproj",
    )(x2, mod, norm_w, w_t, w_t)


class _ScoreMasks:
    def __init__(self, c):
        sub = lax.broadcasted_iota(jnp.int32, (SUBLANES, c), 0)
        self.lane = lax.broadcasted_iota(jnp.int32, (SUBLANES, c), 1)
        in_tile = self.lane & (SUBLANES - 1)
        self.low_bit = sub ^ in_tile
        self.diag = jnp.where(in_tile < sub, self.lane >> int(math.log2(SUBLANES)), -1)

    def level(self, g):
        return (self.low_bit & -g) == g


def _group_mid_rows(p, g):
    c, dk = p.shape
    p3 = p.reshape(c // SUBLANES, SUBLANES, dk)
    sub = lax.broadcasted_iota(jnp.int32, p3.shape, 1)
    out = None
    for s in range(0, SUBLANES, 2 * g):
        part = jnp.broadcast_to(p3[:, s + g - 1:s + g, :], p3.shape)
        out = part if out is None else jnp.where(sub >= s, part, out)
    return out.reshape(c, dk)


def _chunk_attend(q, k, g2, v_heads, st_refs):
    c, width = q.shape
    heads = len(v_heads)
    dk = width // heads
    groups = c // SUBLANES
    hk = lambda a, h: a[:, h * dk:(h + 1) * dk]
    rows = lambda a, r: a[r * SUBLANES:(r + 1) * SUBLANES, :]
    row = lax.broadcasted_iota(jnp.int32, (c, width), 0)
    masks = _ScoreMasks(c)

    p = g2
    small = [[None] * groups for _ in range(heads)]
    g = c // 2
    plan = []
    while g >= SUBLANES:
        plan.append(g)
        g //= 2

    g = 1
    big_scores = {}
    while g < c:
        if g < SUBLANES:
            second = (row & g) != 0
            if g == 1:
                t = pltpu.roll(p, 1, axis=0)
                expo = jnp.where(second, p, 0.0)
            else:
                t = _group_mid_rows(p, g)
                expo = jnp.where(second, p, t - p)
            z = jnp.where(second, q, k) * jnp.exp2(expo)
            p = p + jnp.where(second, t, 0.0)
            zb = z.astype(BF16)
            lvl = None if g == 1 else masks.level(g)
            for h in range(heads):
                s = _dot_nt(hk(zb, h), hk(zb, h))
                for r in range(groups):
                    small[h][r] = rows(s, r) if g == 1 else jnp.where(lvl, rows(s, r), small[h][r])
        else:
            zs, ps = [], []
            for s0 in range(0, c, 2 * g):
                p1, p2 = p[s0:s0 + g, :], p[s0 + g:s0 + 2 * g, :]
                t = jnp.broadcast_to(p1[g - 1:g, :], (g, width))
                zs += [k[s0:s0 + g, :] * jnp.exp2(t - p1), q[s0 + g:s0 + 2 * g, :] * jnp.exp2(p2)]
                ps += [p1, p2 + t]
            zb = jnp.concatenate(zs, axis=0).astype(BF16)
            p = jnp.concatenate(ps, axis=0)
            zq = jnp.concatenate([zb[s0 + g:s0 + 2 * g, :] for s0 in range(0, c, 2 * g)], axis=0)
            big_scores[g] = [_dot_nt(hk(zq, h), hk(zb, h)) for h in range(heads)]
        g *= 2


    att = []
    for h in range(heads):
        out_rows = []
        for r in range(groups):
            first_row = r * SUBLANES
            acc, edge = None, 0
            for g in plan:
                if first_row & g:
                    blk = (first_row // (2 * g)) * g + (first_row % g)
                    piece = big_scores[g][h][blk:blk + SUBLANES, :]
                    acc = piece if acc is None else jnp.where(masks.lane < edge, acc, piece)
                    edge += g
            if acc is None:
                acc = jnp.zeros((SUBLANES, c), F32)
            else:
                acc = jnp.where(masks.lane < edge, acc, 0.0)
            out_rows.append(jnp.where(masks.diag == r, small[h][r], acc))
        att.append(jnp.concatenate(out_rows, axis=0).astype(BF16))

    qk = q * k
    q_dec = (q * jnp.exp2(p)).astype(BF16)
    p_last = p[c - 1:c, :]
    k_dec = (k * jnp.exp2(p_last - p)).astype(BF16)
    st_scale = jnp.exp2(p_last)
    outs = []
    for h in range(heads):
        v = v_heads[h]
        vb = v.astype(BF16)
        st = st_refs[h][...]
        o = _dot(att[h], vb)
        o = o + jnp.sum(hk(qk, h), axis=-1, keepdims=True) * v
        o = o + _dot_nt(hk(q_dec, h), st.astype(BF16))
        st_refs[h][...] = st * hk(st_scale, h) + _dot_tn(vb, hk(k_dec, h))
        outs.append(o)
    return outs


def _gated_head_norm(o, gate, w):
    return _rms_rows(o) * w * (gate * _sigmoid(gate))


def _mixers_kernel(qa_ref, fa_ref, ia_ref, ga_ref, qb_ref, kb_ref, vb_ref, gb_ref, gkl_ref,
                   lbp_ref, wgk_ref, bgk_ref, anw_ref, bnw_ref, wua_ref, wub_ref, wo_ref,
                   ya_ref, yb_ref, wua_bf_ref, wub_bf_ref, wo_bf_ref, sta_ref, stb_ref):
    @pl.when(pl.program_id(1) == 0)
    def _():
        sta_ref[...] = jnp.zeros_like(sta_ref)
        stb_ref[...] = jnp.zeros_like(stb_ref)

    for src, dst in ((wua_ref, wua_bf_ref), (wub_ref, wub_bf_ref), (wo_ref, wo_bf_ref)):
        dst[...] = src[...].astype(dst.dtype)

    lbp = lbp_ref[...]
    pe = jnp.exp(lbp - jnp.max(lbp, axis=0, keepdims=True))
    lb = pe[0:1, :] / jnp.sum(pe, axis=0, keepdims=True)
    anw, bnw = anw_ref[...], bnw_ref[...]
    a_cols = [slice(h * A_HEAD_DIM, (h + 1) * A_HEAD_DIM) for h in range(A_HEADS)]
    b_cols = [slice(h * B_VAL_DIM, (h + 1) * B_VAL_DIM) for h in range(B_HEADS)]
    st_refs = [sta_ref.at[h] for h in range(A_HEADS)] + [stb_ref.at[h] for h in range(B_HEADS)]

    for ci in range(CHUNKS_PER_STEP):
        rows = slice(ci * CHUNK, (ci + 1) * CHUNK)
        qa = qa_ref[rows, :]
        f = lb + (1.0 - lb) * _sigmoid(fa_ref[rows, :])
        z2 = (_dot(gkl_ref[rows, :].astype(BF16), wgk_ref[...]) + bgk_ref[...]) * LOG2E
        gk = (jnp.minimum(z2, 0.0) - jnp.log2(1.0 + jnp.exp2(-jnp.abs(z2)))) * (1.0 / GATE_LOGIT_NORMALIZER)

        q = jnp.concatenate([qa * _sigmoid(qa), qb_ref[rows, :] * (B_KEY_DIM ** -0.5)], axis=1)
        k = jnp.concatenate([1.0 - f, kb_ref[rows, :]], axis=1)
        g2 = jnp.concatenate([jnp.log2(f), gk], axis=1)
        v_heads = [ia_ref[rows, sl] for sl in a_cols] + [vb_ref[rows, sl] for sl in b_cols]
        outs = _chunk_attend(q, k, g2, v_heads, st_refs)

        for sl, o in zip(a_cols, outs[:A_HEADS]):
            ya_ref[rows, sl] = _gated_head_norm(o, ga_ref[rows, sl], anw).astype(ya_ref.dtype)
        for sl, o in zip(b_cols, outs[A_HEADS:]):
            yb_ref[rows, sl] = _gated_head_norm(o, gb_ref[rows, sl], bnw).astype(yb_ref.dtype)


def _mixers(proj, gk_low, lb_param, w_gk2p, b_gk2, a_norm_w, b_norm_w, merge_weights, bsz, seq):
    m = proj.shape[0]
    tm = CHUNK * CHUNKS_PER_STEP
    nc = seq // tm
    steps = bsz * nc
    assert A_HEAD_DIM == B_KEY_DIM
    assert all(w.shape[0] % (steps * 16) == 0 for w in merge_weights)

    def share(w):
        return pl.BlockSpec((w.shape[0] // steps, w.shape[1]), lambda b, c: (b * nc + c, 0))

    def col(c0, width):
        assert c0 % width == 0
        return pl.BlockSpec((tm, width), lambda b, c: (b * nc + c, c0 // width))

    const = lambda b, c: (0, 0)
    return pl.pallas_call(
        _mixers_kernel,
        grid=(bsz, nc),
        in_specs=[
            col(COL_QA, A_WIDTH), col(COL_FA, A_WIDTH), col(COL_IA, A_WIDTH), col(COL_GA, A_WIDTH),
            col(COL_QB, B_KEY_WIDTH), col(COL_KB, B_KEY_WIDTH),
            col(COL_VB, B_VAL_WIDTH), col(COL_GB, B_VAL_WIDTH),
            col(0, LANES),
            pl.BlockSpec(lb_param.shape, const),
            pl.BlockSpec((LANES, B_KEY_WIDTH), const),
            pl.BlockSpec((1, B_KEY_WIDTH), const),
            pl.BlockSpec((1, A_HEAD_DIM), const),
            pl.BlockSpec((1, B_VAL_DIM), const),
        ] + [share(w) for w in merge_weights],
        out_specs=[col(0, A_WIDTH), col(0, B_VAL_WIDTH)] + [share(w) for w in merge_weights],
        out_shape=[
            jax.ShapeDtypeStruct((m, A_WIDTH), BF16),
            jax.ShapeDtypeStruct((m, B_VAL_WIDTH), BF16),
        ] + [jax.ShapeDtypeStruct(w.shape, BF16) for w in merge_weights],
        scratch_shapes=[
            pltpu.VMEM((A_HEADS, A_HEAD_DIM, A_HEAD_DIM), F32),
            pltpu.VMEM((B_HEADS, B_VAL_DIM, B_KEY_DIM), F32),
        ],
        compiler_params=pltpu.CompilerParams(
            dimension_semantics=("arbitrary", "arbitrary"), vmem_limit_bytes=VMEM_LIMIT),
        name="mixers",
    )(proj, proj, proj, proj, proj, proj, proj, proj, gk_low, lb_param, w_gk2p, b_gk2, a_norm_w, b_norm_w,
      *merge_weights)


def _merge_kernel(ya_ref, yb_ref, ma_ref, mb_ref, x_ref, mod_ref, nw_ref,
                  wua_ref, wub_ref, wo_ref, x1_ref, h2_ref):
    ua = _dot(ya_ref[...], wua_ref[...])
    ub = _dot(yb_ref[...], wub_ref[...])
    merged = _sigmoid(ma_ref[...]) * ua + _sigmoid(mb_ref[...]) * ub
    out = _dot(merged.astype(BF16), wo_ref[...])
    x1 = x_ref[...] + mod_ref[0, 2:3, :] * out
    x1_ref[...] = x1
    scale = nw_ref[...] * (1.0 + mod_ref[0, 4:5, :])
    h2_ref[...] = (_rms_rows(x1) * scale + mod_ref[0, 3:4, :]).astype(h2_ref.dtype)


def _merge(ya, yb, proj, x2, mod, norm_w, w_up_a, w_up_b, w_o, seq):
    m, d = x2.shape
    assert COL_MA % d == 0 and COL_MB % d == 0
    tiles_per_seq = seq // TM_MERGE
    row = lambda i: (i, 0)
    const = lambda i: (0, 0)
    return pl.pallas_call(
        _merge_kernel,
        grid=(m // TM_MERGE,),
        in_specs=[
            pl.BlockSpec((TM_MERGE, A_WIDTH), row),
            pl.BlockSpec((TM_MERGE, B_VAL_WIDTH), row),
            pl.BlockSpec((TM_MERGE, d), lambda i: (i, COL_MA // d)),
            pl.BlockSpec((TM_MERGE, d), lambda i: (i, COL_MB // d)),
            pl.BlockSpec((TM_MERGE, d), row),
            pl.BlockSpec((1, N_MOD, d), lambda i: (i // tiles_per_seq, 0, 0)),
            pl.BlockSpec((1, d), const),
            pl.BlockSpec(w_up_a.shape, const),
            pl.BlockSpec(w_up_b.shape, const),
            pl.BlockSpec(w_o.shape, const),
        ],
        out_specs=[pl.BlockSpec((TM_MERGE, d), row), pl.BlockSpec((TM_MERGE, d), row)],
        out_shape=[jax.ShapeDtypeStruct((m, d), F32), jax.ShapeDtypeStruct((m, d), BF16)],
        compiler_params=pltpu.CompilerParams(
            dimension_semantics=("arbitrary",), vmem_limit_bytes=VMEM_LIMIT),
        name="merge",
    )(ya, yb, proj, proj, x2, mod, norm_w, w_up_a, w_up_b, w_o)


def _ffn_up_kernel(h_ref, wg_ref, wu_ref, wdown_ref, o_ref, wdown_bf_ref):
    for r in range(0, o_ref.shape[0], FFN_UP_SUBROWS):
        rs = slice(r, r + FFN_UP_SUBROWS)
        h = h_ref[rs, :]
        for t in range(0, o_ref.shape[1], FFN_UP_SUBTILE):
            sl = slice(t, t + FFN_UP_SUBTILE)
            gate = _dot(h, wg_ref[:, sl].astype(BF16))
            up = _dot(h, wu_ref[:, sl].astype(BF16))
            o_ref[rs, sl] = (gate * _sigmoid(gate) * up).astype(o_ref.dtype)
    wdown_bf_ref[...] = wdown_ref[...].astype(wdown_bf_ref.dtype)


def _ffn_up(h2, w_in, w_down):
    m, d = h2.shape
    hidden = w_in.shape[1] // 2
    nf = hidden // TF_FFN
    steps = (m // TM_FFN_UP) * nf
    assert w_down.shape == (hidden, d) and hidden % (steps * 16) == 0
    share = hidden // steps
    return pl.pallas_call(
        _ffn_up_kernel,
        grid=(m // TM_FFN_UP, nf),
        in_specs=[
            pl.BlockSpec((TM_FFN_UP, d), lambda i, j: (i, 0)),
            pl.BlockSpec((d, TF_FFN), lambda i, j: (0, j)),
            pl.BlockSpec((d, TF_FFN), lambda i, j: (0, j + nf)),
            pl.BlockSpec((share, d), lambda i, j: (i * nf + j, 0)),
        ],
        out_specs=[
            pl.BlockSpec((TM_FFN_UP, TF_FFN), lambda i, j: (i, j)),
            pl.BlockSpec((share, d), lambda i, j: (i * nf + j, 0)),
        ],
        out_shape=[
            jax.ShapeDtypeStruct((m, hidden), BF16),
            jax.ShapeDtypeStruct((hidden, d), BF16),
        ],
        compiler_params=pltpu.CompilerParams(
            dimension_semantics=("arbitrary", "arbitrary"), vmem_limit_bytes=VMEM_LIMIT),
        name="ffn_up",
    )(h2, w_in, w_in, w_down)


def _ffn_down_kernel(a_ref, wo_ref, x1_ref, mod_ref, nw_ref, o_ref):
    x2 = x1_ref[...] + mod_ref[0, 5:6, :] * _dot(a_ref[...], wo_ref[...])
    o_ref[...] = _rms_rows(x2) * nw_ref[...]


def _ffn_down(act, w_out, x1, mod, norm_w, seq):
    m, d = x1.shape
    hidden = w_out.shape[0]
    tiles_per_seq = seq // TM_FFN_DOWN
    return pl.pallas_call(
        _ffn_down_kernel,
        grid=(m // TM_FFN_DOWN,),
        in_specs=[
            pl.BlockSpec((TM_FFN_DOWN, hidden), lambda i: (i, 0)),
            pl.BlockSpec((hidden, d), lambda i: (0, 0)),
            pl.BlockSpec((TM_FFN_DOWN, d), lambda i: (i, 0)),
            pl.BlockSpec((1, N_MOD, d), lambda i: (i // tiles_per_seq, 0, 0)),
            pl.BlockSpec((1, d), lambda i: (0, 0)),
        ],
        out_specs=pl.BlockSpec((TM_FFN_DOWN, d), lambda i: (i, 0)),
        out_shape=jax.ShapeDtypeStruct((m, d), F32),
        compiler_params=pltpu.CompilerParams(
            dimension_semantics=("arbitrary",), vmem_limit_bytes=VMEM_LIMIT),
        name="ffn_down",
    )(act, w_out, x1, mod, norm_w)


def kernel(x, c, w_ada, b_ada, norm1_w, w_in, w_gk2, b_gk2, lb_param, a_norm_w, b_norm_w,
           w_up_a, w_up_b, w_o, norm2_w, w_ffn_in, w_ffn_out, final_norm_w):
    bsz, seq, d = x.shape
    depth = w_in.shape[0]
    assert depth == 1 and d == D_MODEL and w_in.shape[2] == MAIN_WIDTH + GK_RANK
    assert seq % TM_PROJ == 0 and seq % (CHUNK * CHUNKS_PER_STEP) == 0 and seq % TM_MERGE == 0
    assert seq % TM_FFN_UP == 0 and seq % TM_FFN_DOWN == 0
    m = bsz * seq
    x2 = x.reshape(m, d)

    def layer0(w):
        return w.reshape(w.shape[1:])

    mod = _ada(c, layer0(w_ada), b_ada).reshape(bsz, N_MOD, d)

    w_in_t = jnp.swapaxes(layer0(w_in), 0, 1).astype(BF16)
    proj, gk_low = _inproj(x2, mod, norm1_w, w_in_t, seq)

    w_gk2p = jnp.pad(layer0(w_gk2).astype(BF16), ((0, LANES - GK_RANK), (0, 0)))
    ya, yb, wua, wub, wo = _mixers(proj, gk_low, lb_param, w_gk2p, b_gk2, a_norm_w, b_norm_w,
                                   (layer0(w_up_a), layer0(w_up_b), layer0(w_o)), bsz, seq)

    x1, h2 = _merge(ya, yb, proj, x2, mod, norm2_w, wua, wub, wo, seq)

    act, w_down = _ffn_up(h2, layer0(w_ffn_in), layer0(w_ffn_out))
    out = _ffn_down(act, w_down, x1, mod, final_norm_w.reshape(1, d), seq)
    return out.reshape(bsz, seq, d)
```

```python
import functools
import math

import jax
import jax.numpy as jnp
from jax import lax
from jax.experimental import pallas as pl
from jax.experimental.pallas import tpu as pltpu

F32 = jnp.float32
BF16 = jnp.bfloat16

D_MODEL = 2048
A_HEADS = 8
A_HEAD_DIM = 128
A_WIDTH = A_HEADS * A_HEAD_DIM
B_HEADS = 4
B_KEY_DIM = 128
B_VAL_DIM = 256
B_KEY_WIDTH = B_HEADS * B_KEY_DIM
B_VAL_WIDTH = B_HEADS * B_VAL_DIM
GK_RANK = 16
GATE_LOGIT_NORMALIZER = 16.0
N_MOD = 6
EPS = 1e-6
LOG2E = math.log2(math.e)

LANES = 128
SUBLANES = 8
VMEM_LIMIT = 56 * 1024 * 1024
VMEM_LIMIT_PROJ = 60 * 1024 * 1024

MAIN_WIDTH = 4 * A_WIDTH + 2 * B_KEY_WIDTH + 2 * B_VAL_WIDTH + 2 * D_MODEL
COL_MA, COL_MB = 0, D_MODEL
COL_QA = 2 * D_MODEL
COL_FA, COL_IA, COL_GA = COL_QA + A_WIDTH, COL_QA + 2 * A_WIDTH, COL_QA + 3 * A_WIDTH
COL_QB = COL_QA + 4 * A_WIDTH
COL_KB = COL_QB + B_KEY_WIDTH
COL_VB = COL_KB + B_KEY_WIDTH
COL_GB = COL_VB + B_VAL_WIDTH

CHUNK = 128
CHUNKS_PER_STEP = 2
TM_PROJ = 1024
TN_PROJ = 2048
TM_MERGE = 256
TM_FFN_UP = 2048
TF_FFN = 512
FFN_UP_SUBTILE = 256
FFN_UP_SUBROWS = 1024
TM_FFN_DOWN = 512
TN_ADA = 1024


def _sigmoid(x):
    return 1.0 / (1.0 + jnp.exp2(x * (-LOG2E)))


def _rms_rows(x):
    return x * lax.rsqrt(jnp.mean(x * x, axis=-1, keepdims=True) + EPS)


def _dot(a, b):
    return jnp.dot(a, b, preferred_element_type=F32)


def _dot_nt(a, b):
    return lax.dot_general(a, b, (((1,), (1,)), ((), ())), preferred_element_type=F32)


def _dot_tn(a, b):
    return lax.dot_general(a, b, (((0,), (0,)), ((), ())), preferred_element_type=F32)


def _ada_kernel(c_ref, w_ref, b_ref, o_ref):
    c = c_ref[...]
    cond = (c * _sigmoid(c)).astype(BF16)
    o_ref[...] = _dot(cond, w_ref[...].astype(BF16)) + b_ref[...]


def _ada(c, w, b):
    bsz, d = c.shape
    n = w.shape[1]
    return pl.pallas_call(
        _ada_kernel,
        grid=(n // TN_ADA,),
        in_specs=[
            pl.BlockSpec((bsz, d), lambda j: (0, 0)),
            pl.BlockSpec((d, TN_ADA), lambda j: (0, j)),
            pl.BlockSpec((1, TN_ADA), lambda j: (0, j)),
        ],
        out_specs=pl.BlockSpec((bsz, TN_ADA), lambda j: (0, j)),
        out_shape=jax.ShapeDtypeStruct((bsz, n), F32),
        compiler_params=pltpu.CompilerParams(
            dimension_semantics=("arbitrary",), vmem_limit_bytes=VMEM_LIMIT),
        name="ada",
    )(c, w, b)


def _inproj_kernel(first_cols, xa_ref, xb_ref, mod_ref, nw_ref, w_ref, wgk_ref, o_ref, gk_ref, h_scr):
    j = pl.program_id(1)
    half = xa_ref.shape[0]

    @pl.when(j == 0)
    def _():
        scale = nw_ref[...] * (1.0 + mod_ref[0, 1:2, :])
        for x_ref, rs in ((xa_ref, slice(0, half)), (xb_ref, slice(half, 2 * half))):
            hb = (_rms_rows(x_ref[...]) * scale + mod_ref[0, 0:1, :]).astype(BF16)
            h_scr[rs, :] = hb
            gk_ref[rs, :] = _dot_nt(hb, wgk_ref[...])
        o_ref[:, :first_cols] = _dot_nt(h_scr[...], w_ref[:first_cols, :])
        if first_cols < o_ref.shape[1]:
            o_ref[:, first_cols:] = jnp.zeros((o_ref.shape[0], o_ref.shape[1] - first_cols), o_ref.dtype)

    @pl.when(j > 0)
    def _():
        o_ref[...] = _dot_nt(h_scr[...], w_ref[...])


def _inproj(x2, mod, norm_w, w_t, seq):
    m, d = x2.shape
    n_mix = MAIN_WIDTH - 2 * d
    gate0 = n_mix + GK_RANK
    assert w_t.shape == (MAIN_WIDTH + GK_RANK, d) and (2 * d) % TN_PROJ == 0
    assert gate0 % 16 == 0
    ng = 2 * d // TN_PROJ
    nm = pl.cdiv(n_mix, TN_PROJ)
    first_cols = n_mix - (nm - 1) * TN_PROJ
    assert nm * TN_PROJ <= w_t.shape[0]
    steps = ng + nm
    assert steps >= 5
    n_row_tiles = m // TM_PROJ
    tiles_per_seq = seq // TM_PROJ
    half = TM_PROJ // 2

    def w_rows(i, j):
        t = j - 1
        rows = jnp.where(t < ng, gate0 + t * TN_PROJ, (t - ng) * TN_PROJ)
        return (pl.multiple_of(jnp.where(j == 0, (nm - 1) * TN_PROJ, rows), 16), 0)

    def out_cols(i, j):
        return (i, jnp.where(j == 0, steps - 1, j - 1))

    def x_half(which, early_step):
        return pl.BlockSpec(
            (half, d),
            lambda i, j: (2 * jnp.minimum(i + (j >= early_step).astype(jnp.int32), n_row_tiles - 1) + which, 0))

    return pl.pallas_call(
        functools.partial(_inproj_kernel, first_cols),
        grid=(n_row_tiles, steps),
        in_specs=[
            x_half(0, steps - 3),
            x_half(1, steps - 2),
            pl.BlockSpec((1, N_MOD, d), lambda i, j: (i // tiles_per_seq, 0, 0)),
            pl.BlockSpec((1, d), lambda i, j: (0, 0)),
            pl.BlockSpec((pl.Element(TN_PROJ), pl.Element(d)), w_rows),
            pl.BlockSpec((pl.Element(LANES), pl.Element(d)), lambda i, j: (n_mix, 0)),
        ],
        out_specs=[
            pl.BlockSpec((TM_PROJ, TN_PROJ), out_cols),
            pl.BlockSpec((TM_PROJ, LANES), lambda i, j: (i, 0)),
        ],
        out_shape=[
            jax.ShapeDtypeStruct((m, steps * TN_PROJ), F32),
            jax.ShapeDtypeStruct((m, LANES), F32),
        ],
        scratch_shapes=[pltpu.VMEM((TM_PROJ, d), BF16)],
        compiler_params=pltpu.CompilerParams(
            dimension_semantics=("arbitrary", "arbitrary"), vmem_limit_bytes=VMEM_LIMIT_PROJ),
        name="inproj",
    )(x2, x2, mod, norm_w, w_t, w_t)


class _ScoreMasks:
    def __init__(self, c):
        sub = lax.broadcasted_iota(jnp.int32, (SUBLANES, c), 0)
        self.lane = lax.broadcasted_iota(jnp.int32, (SUBLANES, c), 1)
        in_tile = self.lane & (SUBLANES - 1)
        self.low_bit = sub ^ in_tile
        self.diag = jnp.where(in_tile < sub, self.lane >> int(math.log2(SUBLANES)), -1)

    def level(self, g):
        return (self.low_bit & -g) == g


def _group_mid_rows(p, g):
    c, dk = p.shape
    p3 = p.reshape(c // SUBLANES, SUBLANES, dk)
    sub = lax.broadcasted_iota(jnp.int32, p3.shape, 1)
    out = None
    for s in range(0, SUBLANES, 2 * g):
        part = jnp.broadcast_to(p3[:, s + g - 1:s + g, :], p3.shape)
        out = part if out is None else jnp.where(sub >= s, part, out)
    return out.reshape(c, dk)


def _chunk_attend(q, k, g2, v_heads, st_refs):
    c, width = q.shape
    heads = len(v_heads)
    dk = width // heads
    groups = c // SUBLANES
    hk = lambda a, h: a[:, h * dk:(h + 1) * dk]
    rows = lambda a, r: a[r * SUBLANES:(r + 1) * SUBLANES, :]
    row = lax.broadcasted_iota(jnp.int32, (c, width), 0)
    masks = _ScoreMasks(c)

    p = g2
    small = [[None] * groups for _ in range(heads)]
    g = c // 2
    plan = []
    while g >= SUBLANES:
        plan.append(g)
        g //= 2

    g = 1
    big_scores = {}
    while g < c:
        if g < SUBLANES:
            second = (row & g) != 0
            if g == 1:
                t = pltpu.roll(p, 1, axis=0)
                expo = jnp.where(second, p, 0.0)
            else:
                t = _group_mid_rows(p, g)
                expo = jnp.where(second, p, t - p)
            z = jnp.where(second, q, k) * jnp.exp2(expo)
            p = p + jnp.where(second, t, 0.0)
            zb = z.astype(BF16)
            lvl = None if g == 1 else masks.level(g)
            for h in range(heads):
                s = _dot_nt(hk(zb, h), hk(zb, h))
                for r in range(groups):
                    small[h][r] = rows(s, r) if g == 1 else jnp.where(lvl, rows(s, r), small[h][r])
        else:
            zs, ps = [], []
            for s0 in range(0, c, 2 * g):
                p1, p2 = p[s0:s0 + g, :], p[s0 + g:s0 + 2 * g, :]
                t = jnp.broadcast_to(p1[g - 1:g, :], (g, width))
                zs += [k[s0:s0 + g, :] * jnp.exp2(t - p1), q[s0 + g:s0 + 2 * g, :] * jnp.exp2(p2)]
                ps += [p1, p2 + t]
            zb = jnp.concatenate(zs, axis=0).astype(BF16)
            p = jnp.concatenate(ps, axis=0)
            zq = jnp.concatenate([zb[s0 + g:s0 + 2 * g, :] for s0 in range(0, c, 2 * g)], axis=0)
            big_scores[g] = [_dot_nt(hk(zq, h), hk(zb, h)) for h in range(heads)]
        g *= 2


    att = []
    for h in range(heads):
        out_rows = []
        for r in range(groups):
            first_row = r * SUBLANES
            acc, edge = None, 0
            for g in plan:
                if first_row & g:
                    blk = (first_row // (2 * g)) * g + (first_row % g)
                    piece = big_scores[g][h][blk:blk + SUBLANES, :]
                    acc = piece if acc is None else jnp.where(masks.lane < edge, acc, piece)
                    edge += g
            if acc is None:
                acc = jnp.zeros((SUBLANES, c), F32)
            else:
                acc = jnp.where(masks.lane < edge, acc, 0.0)
            out_rows.append(jnp.where(masks.diag == r, small[h][r], acc))
        att.append(jnp.concatenate(out_rows, axis=0).astype(BF16))

    qk = q * k
    q_dec = (q * jnp.exp2(p)).astype(BF16)
    p_last = p[c - 1:c, :]
    k_dec = (k * jnp.exp2(p_last - p)).astype(BF16)
    st_scale = jnp.exp2(p_last)
    outs = []
    for h in range(heads):
        v = v_heads[h]
        vb = v.astype(BF16)
        st = st_refs[h][...]
        o = _dot(att[h], vb)
        o = o + jnp.sum(hk(qk, h), axis=-1, keepdims=True) * v
        o = o + _dot_nt(hk(q_dec, h), st.astype(BF16))
        st_refs[h][...] = st * hk(st_scale, h) + _dot_tn(vb, hk(k_dec, h))
        outs.append(o)
    return outs


def _gated_head_norm(o, gate, w):
    return _rms_rows(o) * w * (gate * _sigmoid(gate))


def _mixers_kernel(qa_ref, fa_ref, ia_ref, ga_ref, qb_ref, kb_ref, vb_ref, gb_ref, gkl_ref,
                   lbp_ref, wgk_ref, bgk_ref, anw_ref, bnw_ref, wua_ref, wub_ref, wo_ref,
                   ya_ref, yb_ref, wua_bf_ref, wub_bf_ref, wo_bf_ref, sta_ref, stb_ref):
    @pl.when(pl.program_id(1) == 0)
    def _():
        sta_ref[...] = jnp.zeros_like(sta_ref)
        stb_ref[...] = jnp.zeros_like(stb_ref)

    for src, dst in ((wua_ref, wua_bf_ref), (wub_ref, wub_bf_ref), (wo_ref, wo_bf_ref)):
        dst[...] = src[...].astype(dst.dtype)

    lbp = lbp_ref[...]
    pe = jnp.exp(lbp - jnp.max(lbp, axis=0, keepdims=True))
    lb = pe[0:1, :] / jnp.sum(pe, axis=0, keepdims=True)
    anw, bnw = anw_ref[...], bnw_ref[...]
    a_cols = [slice(h * A_HEAD_DIM, (h + 1) * A_HEAD_DIM) for h in range(A_HEADS)]
    b_cols = [slice(h * B_VAL_DIM, (h + 1) * B_VAL_DIM) for h in range(B_HEADS)]
    st_refs = [sta_ref.at[h] for h in range(A_HEADS)] + [stb_ref.at[h] for h in range(B_HEADS)]

    for ci in range(CHUNKS_PER_STEP):
        rows = slice(ci * CHUNK, (ci + 1) * CHUNK)
        qa = qa_ref[rows, :]
        f = lb + (1.0 - lb) * _sigmoid(fa_ref[rows, :])
        z2 = (_dot(gkl_ref[rows, :].astype(BF16), wgk_ref[...]) + bgk_ref[...]) * LOG2E
        gk = (jnp.minimum(z2, 0.0) - jnp.log2(1.0 + jnp.exp2(-jnp.abs(z2)))) * (1.0 / GATE_LOGIT_NORMALIZER)

        q = jnp.concatenate([qa * _sigmoid(qa), qb_ref[rows, :] * (B_KEY_DIM ** -0.5)], axis=1)
        k = jnp.concatenate([1.0 - f, kb_ref[rows, :]], axis=1)
        g2 = jnp.concatenate([jnp.log2(f), gk], axis=1)
        v_heads = [ia_ref[rows, sl] for sl in a_cols] + [vb_ref[rows, sl] for sl in b_cols]
        outs = _chunk_attend(q, k, g2, v_heads, st_refs)

        for sl, o in zip(a_cols, outs[:A_HEADS]):
            ya_ref[rows, sl] = _gated_head_norm(o, ga_ref[rows, sl], anw).astype(ya_ref.dtype)
        for sl, o in zip(b_cols, outs[A_HEADS:]):
            yb_ref[rows, sl] = _gated_head_norm(o, gb_ref[rows, sl], bnw).astype(yb_ref.dtype)


def _mixers(proj, gk_low, lb_param, w_gk2p, b_gk2, a_norm_w, b_norm_w, merge_weights, bsz, seq):
    m = proj.shape[0]
    tm = CHUNK * CHUNKS_PER_STEP
    nc = seq // tm
    steps = bsz * nc
    assert A_HEAD_DIM == B_KEY_DIM
    assert all(w.shape[0] % (steps * 16) == 0 for w in merge_weights)

    def share(w):
        return pl.BlockSpec((w.shape[0] // steps, w.shape[1]), lambda b, c: (b * nc + c, 0))

    def col(c0, width):
        assert c0 % width == 0
        return pl.BlockSpec((tm, width), lambda b, c: (b * nc + c, c0 // width))

    const = lambda b, c: (0, 0)
    return pl.pallas_call(
        _mixers_kernel,
        grid=(bsz, nc),
        in_specs=[
            col(COL_QA, A_WIDTH), col(COL_FA, A_WIDTH), col(COL_IA, A_WIDTH), col(COL_GA, A_WIDTH),
            col(COL_QB, B_KEY_WIDTH), col(COL_KB, B_KEY_WIDTH),
            col(COL_VB, B_VAL_WIDTH), col(COL_GB, B_VAL_WIDTH),
            col(0, LANES),
            pl.BlockSpec(lb_param.shape, const),
            pl.BlockSpec((LANES, B_KEY_WIDTH), const),
            pl.BlockSpec((1, B_KEY_WIDTH), const),
            pl.BlockSpec((1, A_HEAD_DIM), const),
            pl.BlockSpec((1, B_VAL_DIM), const),
        ] + [share(w) for w in merge_weights],
        out_specs=[col(0, A_WIDTH), col(0, B_VAL_WIDTH)] + [share(w) for w in merge_weights],
        out_shape=[
            jax.ShapeDtypeStruct((m, A_WIDTH), BF16),
            jax.ShapeDtypeStruct((m, B_VAL_WIDTH), BF16),
        ] + [jax.ShapeDtypeStruct(w.shape, BF16) for w in merge_weights],
        scratch_shapes=[
            pltpu.VMEM((A_HEADS, A_HEAD_DIM, A_HEAD_DIM), F32),
            pltpu.VMEM((B_HEADS, B_VAL_DIM, B_KEY_DIM), F32),
        ],
        compiler_params=pltpu.CompilerParams(
            dimension_semantics=("arbitrary", "arbitrary"), vmem_limit_bytes=VMEM_LIMIT),
        name="mixers",
    )(proj, proj, proj, proj, proj, proj, proj, proj, gk_low, lb_param, w_gk2p, b_gk2, a_norm_w, b_norm_w,
      *merge_weights)


def _merge_kernel(ya_ref, yb_ref, ma_ref, mb_ref, x_ref, mod_ref, nw_ref,
                  wua_ref, wub_ref, wo_ref, x1_ref, h2_ref):
    ua = _dot(ya_ref[...], wua_ref[...])
    ub = _dot(yb_ref[...], wub_ref[...])
    merged = _sigmoid(ma_ref[...]) * ua + _sigmoid(mb_ref[...]) * ub
    out = _dot(merged.astype(BF16), wo_ref[...])
    x1 = x_ref[...] + mod_ref[0, 2:3, :] * out
    x1_ref[...] = x1
    scale = nw_ref[...] * (1.0 + mod_ref[0, 4:5, :])
    h2_ref[...] = (_rms_rows(x1) * scale + mod_ref[0, 3:4, :]).astype(h2_ref.dtype)


def _merge(ya, yb, proj, x2, mod, norm_w, w_up_a, w_up_b, w_o, seq):
    m, d = x2.shape
    assert COL_MA % d == 0 and COL_MB % d == 0
    tiles_per_seq = seq // TM_MERGE
    row = lambda i: (i, 0)
    const = lambda i: (0, 0)
    return pl.pallas_call(
        _merge_kernel,
        grid=(m // TM_MERGE,),
        in_specs=[
            pl.BlockSpec((TM_MERGE, A_WIDTH), row),
            pl.BlockSpec((TM_MERGE, B_VAL_WIDTH), row),
            pl.BlockSpec((TM_MERGE, d), lambda i: (i, COL_MA // d)),
            pl.BlockSpec((TM_MERGE, d), lambda i: (i, COL_MB // d)),
            pl.BlockSpec((TM_MERGE, d), row),
            pl.BlockSpec((1, N_MOD, d), lambda i: (i // tiles_per_seq, 0, 0)),
            pl.BlockSpec((1, d), const),
            pl.BlockSpec(w_up_a.shape, const),
            pl.BlockSpec(w_up_b.shape, const),
            pl.BlockSpec(w_o.shape, const),
        ],
        out_specs=[pl.BlockSpec((TM_MERGE, d), row), pl.BlockSpec((TM_MERGE, d), row)],
        out_shape=[jax.ShapeDtypeStruct((m, d), F32), jax.ShapeDtypeStruct((m, d), BF16)],
        compiler_params=pltpu.CompilerParams(
            dimension_semantics=("arbitrary",), vmem_limit_bytes=VMEM_LIMIT),
        name="merge",
    )(ya, yb, proj, proj, x2, mod, norm_w, w_up_a, w_up_b, w_o)


def _ffn_up_kernel(h_ref, wg_ref, wu_ref, wdown_ref, o_ref, wdown_bf_ref):
    for r in range(0, o_ref.shape[0], FFN_UP_SUBROWS):
        rs = slice(r, r + FFN_UP_SUBROWS)
        h = h_ref[rs, :]
        for t in range(0, o_ref.shape[1], FFN_UP_SUBTILE):
            sl = slice(t, t + FFN_UP_SUBTILE)
            gate = _dot(h, wg_ref[:, sl].astype(BF16))
            up = _dot(h, wu_ref[:, sl].astype(BF16))
            o_ref[rs, sl] = (gate * _sigmoid(gate) * up).astype(o_ref.dtype)
    wdown_bf_ref[...] = wdown_ref[...].astype(wdown_bf_ref.dtype)


def _ffn_up(h2, w_in, w_down):
    m, d = h2.shape
    hidden = w_in.shape[1] // 2
    nf = hidden // TF_FFN
    steps = (m // TM_FFN_UP) * nf
    assert w_down.shape == (hidden, d) and hidden % (steps * 16) == 0
    share = hidden // steps
    return pl.pallas_call(
        _ffn_up_kernel,
        grid=(m // TM_FFN_UP, nf),
        in_specs=[
            pl.BlockSpec((TM_FFN_UP, d), lambda i, j: (i, 0)),
            pl.BlockSpec((d, TF_FFN), lambda i, j: (0, j)),
            pl.BlockSpec((d, TF_FFN), lambda i, j: (0, j + nf)),
            pl.BlockSpec((share, d), lambda i, j: (i * nf + j, 0)),
        ],
        out_specs=[
            pl.BlockSpec((TM_FFN_UP, TF_FFN), lambda i, j: (i, j)),
            pl.BlockSpec((share, d), lambda i, j: (i * nf + j, 0)),
        ],
        out_shape=[
            jax.ShapeDtypeStruct((m, hidden), BF16),
            jax.ShapeDtypeStruct((hidden, d), BF16),
        ],
        compiler_params=pltpu.CompilerParams(
            dimension_semantics=("arbitrary", "arbitrary"), vmem_limit_bytes=VMEM_LIMIT),
        name="ffn_up",
    )(h2, w_in, w_in, w_down)


def _ffn_down_kernel(a_ref, wo_ref, x1_ref, mod_ref, nw_ref, o_ref):
    x2 = x1_ref[...] + mod_ref[0, 5:6, :] * _dot(a_ref[...], wo_ref[...])
    o_ref[...] = _rms_rows(x2) * nw_ref[...]


def _ffn_down(act, w_out, x1, mod, norm_w, seq):
    m, d = x1.shape
    hidden = w_out.shape[0]
    tiles_per_seq = seq // TM_FFN_DOWN
    return pl.pallas_call(
        _ffn_down_kernel,
        grid=(m // TM_FFN_DOWN,),
        in_specs=[
            pl.BlockSpec((TM_FFN_DOWN, hidden), lambda i: (i, 0)),
            pl.BlockSpec((hidden, d), lambda i: (0, 0)),
            pl.BlockSpec((TM_FFN_DOWN, d), lambda i: (i, 0)),
            pl.BlockSpec((1, N_MOD, d), lambda i: (i // tiles_per_seq, 0, 0)),
            pl.BlockSpec((1, d), lambda i: (0, 0)),
        ],
        out_specs=pl.BlockSpec((TM_FFN_DOWN, d), lambda i: (i, 0)),
        out_shape=jax.ShapeDtypeStruct((m, d), F32),
        compiler_params=pltpu.CompilerParams(
            dimension_semantics=("arbitrary",), vmem_limit_bytes=VMEM_LIMIT),
        name="ffn_down",
    )(act, w_out, x1, mod, norm_w)


def kernel(x, c, w_ada, b_ada, norm1_w, w_in, w_gk2, b_gk2, lb_param, a_norm_w, b_norm_w,
           w_up_a, w_up_b, w_o, norm2_w, w_ffn_in, w_ffn_out, final_norm_w):
    bsz, seq, d = x.shape
    depth = w_in.shape[0]
    assert depth == 1 and d == D_MODEL and w_in.shape[2] == MAIN_WIDTH + GK_RANK
    assert seq % TM_PROJ == 0 and seq % (CHUNK * CHUNKS_PER_STEP) == 0 and seq % TM_MERGE == 0
    assert seq % TM_FFN_UP == 0 and seq % TM_FFN_DOWN == 0
    m = bsz * seq
    x2 = x.reshape(m, d)

    def layer0(w):
        return w.reshape(w.shape[1:])

    mod = _ada(c, layer0(w_ada), b_ada).reshape(bsz, N_MOD, d)

    w_in_t = jnp.swapaxes(layer0(w_in), 0, 1).astype(BF16)
    proj, gk_low = _inproj(x2, mod, norm1_w, w_in_t, seq)

    w_gk2p = jnp.pad(layer0(w_gk2).astype(BF16), ((0, LANES - GK_RANK), (0, 0)))
    ya, yb, wua, wub, wo = _mixers(proj, gk_low, lb_param, w_gk2p, b_gk2, a_norm_w, b_norm_w,
                                   (layer0(w_up_a), layer0(w_up_b), layer0(w_o)), bsz, seq)

    x1, h2 = _merge(ya, yb, proj, x2, mod, norm2_w, wua, wub, wo, seq)

    act, w_down = _ffn_up(h2, layer0(w_ffn_in), layer0(w_ffn_out))
    out = _ffn_down(act, w_down, x1, mod, final_norm_w.reshape(1, d), seq)
    return out.reshape(bsz, seq, d)
```

```python
import functools
import math

import jax
import jax.numpy as jnp
from jax import lax
from jax.experimental import pallas as pl
from jax.experimental.pallas import tpu as pltpu

F32 = jnp.float32
BF16 = jnp.bfloat16

D_MODEL = 2048
A_HEADS = 8
A_HEAD_DIM = 128
A_WIDTH = A_HEADS * A_HEAD_DIM
B_HEADS = 4
B_KEY_DIM = 128
B_VAL_DIM = 256
B_KEY_WIDTH = B_HEADS * B_KEY_DIM
B_VAL_WIDTH = B_HEADS * B_VAL_DIM
GK_RANK = 16
GATE_LOGIT_NORMALIZER = 16.0
N_MOD = 6
EPS = 1e-6
LOG2E = math.log2(math.e)

LANES = 128
SUBLANES = 8
VMEM_LIMIT = 56 * 1024 * 1024
VMEM_LIMIT_PROJ = 60 * 1024 * 1024

MAIN_WIDTH = 4 * A_WIDTH + 2 * B_KEY_WIDTH + 2 * B_VAL_WIDTH + 2 * D_MODEL
COL_MA, COL_MB = 0, D_MODEL
COL_QA = 2 * D_MODEL
COL_FA, COL_IA, COL_GA = COL_QA + A_WIDTH, COL_QA + 2 * A_WIDTH, COL_QA + 3 * A_WIDTH
COL_QB = COL_QA + 4 * A_WIDTH
COL_KB = COL_QB + B_KEY_WIDTH
COL_VB = COL_KB + B_KEY_WIDTH
COL_GB = COL_VB + B_VAL_WIDTH

CHUNK = 128
CHUNKS_PER_STEP = 2
TM_PROJ = 1024
TN_PROJ = 2048
TM_MERGE = 256
TM_FFN_UP = 2048
TF_FFN = 512
FFN_UP_SUBTILE = 256
FFN_UP_SUBROWS = 1024
TM_FFN_DOWN = 512
TN_ADA = 1024


def _sigmoid(x):
    return 1.0 / (1.0 + jnp.exp2(x * (-LOG2E)))


def _rms_rows(x):
    return x * lax.rsqrt(jnp.mean(x * x, axis=-1, keepdims=True) + EPS)


def _dot(a, b):
    return jnp.dot(a, b, preferred_element_type=F32)


def _dot_nt(a, b):
    return lax.dot_general(a, b, (((1,), (1,)), ((), ())), preferred_element_type=F32)


def _dot_tn(a, b):
    return lax.dot_general(a, b, (((0,), (0,)), ((), ())), preferred_element_type=F32)


def _ada_kernel(c_ref, w_ref, b_ref, o_ref):
    c = c_ref[...]
    cond = (c * _sigmoid(c)).astype(BF16)
    o_ref[...] = _dot(cond, w_ref[...].astype(BF16)) + b_ref[...]


def _ada(c, w, b):
    bsz, d = c.shape
    n = w.shape[1]
    return pl.pallas_call(
        _ada_kernel,
        grid=(n // TN_ADA,),
        in_specs=[
            pl.BlockSpec((bsz, d), lambda j: (0, 0)),
            pl.BlockSpec((d, TN_ADA), lambda j: (0, j)),
            pl.BlockSpec((1, TN_ADA), lambda j: (0, j)),
        ],
        out_specs=pl.BlockSpec((bsz, TN_ADA), lambda j: (0, j)),
        out_shape=jax.ShapeDtypeStruct((bsz, n), F32),
        compiler_params=pltpu.CompilerParams(
            dimension_semantics=("arbitrary",), vmem_limit_bytes=VMEM_LIMIT),
        name="ada",
    )(c, w, b)


def _inproj_kernel(first_cols, xa_ref, xb_ref, mod_ref, nw_ref, w_ref, wgk_ref, o_ref, gk_ref, h_scr):
    j = pl.program_id(1)
    half = xa_ref.shape[0]

    @pl.when(j == 0)
    def _():
        scale = nw_ref[...] * (1.0 + mod_ref[0, 1:2, :])
        for x_ref, rs in ((xa_ref, slice(0, half)), (xb_ref, slice(half, 2 * half))):
            hb = (_rms_rows(x_ref[...]) * scale + mod_ref[0, 0:1, :]).astype(BF16)
            h_scr[rs, :] = hb
            gk_ref[rs, :] = _dot_nt(hb, wgk_ref[...])
        o_ref[:, :first_cols] = _dot_nt(h_scr[...], w_ref[:first_cols, :])
        if first_cols < o_ref.shape[1]:
            o_ref[:, first_cols:] = jnp.zeros((o_ref.shape[0], o_ref.shape[1] - first_cols), o_ref.dtype)

    @pl.when(j > 0)
    def _():
        o_ref[...] = _dot_nt(h_scr[...], w_ref[...])


def _inproj(x2, mod, norm_w, w_t, seq):
    m, d = x2.shape
    n_mix = MAIN_WIDTH - 2 * d
    gate0 = n_mix + GK_RANK
    assert w_t.shape == (MAIN_WIDTH + GK_RANK, d) and (2 * d) % TN_PROJ == 0
    assert gate0 % 16 == 0
    ng = 2 * d // TN_PROJ
    nm = pl.cdiv(n_mix, TN_PROJ)
    first_cols = n_mix - (nm - 1) * TN_PROJ
    assert nm * TN_PROJ <= w_t.shape[0]
    steps = ng + nm
    assert steps >= 5
    n_row_tiles = m // TM_PROJ
    tiles_per_seq = seq // TM_PROJ
    half = TM_PROJ // 2

    def w_rows(i, j):
        t = j - 1
        rows = jnp.where(t < ng, gate0 + t * TN_PROJ, (t - ng) * TN_PROJ)
        return (pl.multiple_of(jnp.where(j == 0, (nm - 1) * TN_PROJ, rows), 16), 0)

    def out_cols(i, j):
        return (i, jnp.where(j == 0, steps - 1, j - 1))

    def x_half(which, early_step):
        return pl.BlockSpec(
            (half, d),
            lambda i, j: (2 * jnp.minimum(i + (j >= early_step).astype(jnp.int32), n_row_tiles - 1) + which, 0))

    return pl.pallas_call(
        functools.partial(_inproj_kernel, first_cols),
        grid=(n_row_tiles, steps),
        in_specs=[
            x_half(0, steps - 3),
            x_half(1, steps - 2),
            pl.BlockSpec((1, N_MOD, d), lambda i, j: (i // tiles_per_seq, 0, 0)),
            pl.BlockSpec((1, d), lambda i, j: (0, 0)),
            pl.BlockSpec((pl.Element(TN_PROJ), pl.Element(d)), w_rows),
            pl.BlockSpec((pl.Element(LANES), pl.Element(d)), lambda i, j: (n_mix, 0)),
        ],
        out_specs=[
            pl.BlockSpec((TM_PROJ, TN_PROJ), out_cols),
            pl.BlockSpec((TM_PROJ, LANES), lambda i, j: (i, 0)),
        ],
        out_shape=[
            jax.ShapeDtypeStruct((m, steps * TN_PROJ), F32),
            jax.ShapeDtypeStruct((m, LANES), F32),
        ],
        scratch_shapes=[pltpu.VMEM((TM_PROJ, d), BF16)],
        compiler_params=pltpu.CompilerParams(
            dimension_semantics=("arbitrary", "arbitrary"), vmem_limit_bytes=VMEM_LIMIT_PROJ),
        name="inproj",
    )(x2, x2, mod, norm_w, w_t, w_t)


class _ScoreMasks:
    def __init__(self, c):
        sub = lax.broadcasted_iota(jnp.int32, (SUBLANES, c), 0)
        self.lane = lax.broadcasted_iota(jnp.int32, (SUBLANES, c), 1)
        in_tile = self.lane & (SUBLANES - 1)
        self.low_bit = sub ^ in_tile
        self.diag = jnp.where(in_tile < sub, self.lane >> int(math.log2(SUBLANES)), -1)

    def level(self, g):
        return (self.low_bit & -g) == g


def _group_mid_rows(p, g):
    c, dk = p.shape
    p3 = p.reshape(c // SUBLANES, SUBLANES, dk)
    sub = lax.broadcasted_iota(jnp.int32, p3.shape, 1)
    out = None
    for s in range(0, SUBLANES, 2 * g):
        part = jnp.broadcast_to(p3[:, s + g - 1:s + g, :], p3.shape)
        out = part if out is None else jnp.where(sub >= s, part, out)
    return out.reshape(c, dk)


def _chunk_attend(q, k, g2, v_heads, st_refs):
    c, width = q.shape
    heads = len(v_heads)
    dk = width // heads
    groups = c // SUBLANES
    hk = lambda a, h: a[:, h * dk:(h + 1) * dk]
    rows = lambda a, r: a[r * SUBLANES:(r + 1) * SUBLANES, :]
    row = lax.broadcasted_iota(jnp.int32, (c, width), 0)
    masks = _ScoreMasks(c)

    p = g2
    small = [[None] * groups for _ in range(heads)]
    g = c // 2
    plan = []
    while g >= SUBLANES:
        plan.append(g)
        g //= 2

    g = 1
    big_scores = {}
    while g < c:
        if g < SUBLANES:
            second = (row & g) != 0
            if g == 1:
                t = pltpu.roll(p, 1, axis=0)
                expo = jnp.where(second, p, 0.0)
            else:
                t = _group_mid_rows(p, g)
                expo = jnp.where(second, p, t - p)
            z = jnp.where(second, q, k) * jnp.exp2(expo)
            p = p + jnp.where(second, t, 0.0)
            zb = z.astype(BF16)
            lvl = None if g == 1 else masks.level(g)
            for h in range(heads):
                s = _dot_nt(hk(zb, h), hk(zb, h))
                for r in range(groups):
                    small[h][r] = rows(s, r) if g == 1 else jnp.where(lvl, rows(s, r), small[h][r])
        else:
            zs, ps = [], []
            for s0 in range(0, c, 2 * g):
                p1, p2 = p[s0:s0 + g, :], p[s0 + g:s0 + 2 * g, :]
                t = jnp.broadcast_to(p1[g - 1:g, :], (g, width))
                zs += [k[s0:s0 + g, :] * jnp.exp2(t - p1), q[s0 + g:s0 + 2 * g, :] * jnp.exp2(p2)]
                ps += [p1, p2 + t]
            zb = jnp.concatenate(zs, axis=0).astype(BF16)
            p = jnp.concatenate(ps, axis=0)
            zq = jnp.concatenate([zb[s0 + g:s0 + 2 * g, :] for s0 in range(0, c, 2 * g)], axis=0)
            big_scores[g] = [_dot_nt(hk(zq, h), hk(zb, h)) for h in range(heads)]
        g *= 2


    att = []
    for h in range(heads):
        out_rows = []
        for r in range(groups):
            first_row = r * SUBLANES
            acc, edge = None, 0
            for g in plan:
                if first_row & g:
                    blk = (first_row // (2 * g)) * g + (first_row % g)
                    piece = big_scores[g][h][blk:blk + SUBLANES, :]
                    acc = piece if acc is None else jnp.where(masks.lane < edge, acc, piece)
                    edge += g
            if acc is None:
                acc = jnp.zeros((SUBLANES, c), F32)
            else:
                acc = jnp.where(masks.lane < edge, acc, 0.0)
            out_rows.append(jnp.where(masks.diag == r, small[h][r], acc))
        att.append(jnp.concatenate(out_rows, axis=0).astype(BF16))

    qk = q * k
    q_dec = (q * jnp.exp2(p)).astype(BF16)
    p_last = p[c - 1:c, :]
    k_dec = (k * jnp.exp2(p_last - p)).astype(BF16)
    st_scale = jnp.exp2(p_last)
    outs = []
    for h in range(heads):
        v = v_heads[h]
        vb = v.astype(BF16)
        st = st_refs[h][...]
        o = _dot(att[h], vb)
        o = o + jnp.sum(hk(qk, h), axis=-1, keepdims=True) * v
        o = o + _dot_nt(hk(q_dec, h), st.astype(BF16))
        st_refs[h][...] = st * hk(st_scale, h) + _dot_tn(vb, hk(k_dec, h))
        outs.append(o)
    return outs


def _gated_head_norm(o, gate, w):
    return _rms_rows(o) * w * (gate * _sigmoid(gate))


def _mixers_kernel(mix_ref, gkl_ref, lbp_ref, wgk_ref, bgk_ref, anw_ref, bnw_ref, wua_ref, wub_ref, wo_ref,
                   ya_ref, yb_ref, wua_bf_ref, wub_bf_ref, wo_bf_ref, sta_ref, stb_ref):
    group = lambda c0, width: mix_ref.at[:, c0 - COL_QA:c0 - COL_QA + width]
    qa_ref, fa_ref, ia_ref, ga_ref = (group(c0, A_WIDTH) for c0 in (COL_QA, COL_FA, COL_IA, COL_GA))
    qb_ref, kb_ref = group(COL_QB, B_KEY_WIDTH), group(COL_KB, B_KEY_WIDTH)
    vb_ref, gb_ref = group(COL_VB, B_VAL_WIDTH), group(COL_GB, B_VAL_WIDTH)

    @pl.when(pl.program_id(1) == 0)
    def _():
        sta_ref[...] = jnp.zeros_like(sta_ref)
        stb_ref[...] = jnp.zeros_like(stb_ref)

    for src, dst in ((wua_ref, wua_bf_ref), (wub_ref, wub_bf_ref), (wo_ref, wo_bf_ref)):
        dst[...] = src[...].astype(dst.dtype)

    lbp = lbp_ref[...]
    pe = jnp.exp(lbp - jnp.max(lbp, axis=0, keepdims=True))
    lb = pe[0:1, :] / jnp.sum(pe, axis=0, keepdims=True)
    anw, bnw = anw_ref[...], bnw_ref[...]
    a_cols = [slice(h * A_HEAD_DIM, (h + 1) * A_HEAD_DIM) for h in range(A_HEADS)]
    b_cols = [slice(h * B_VAL_DIM, (h + 1) * B_VAL_DIM) for h in range(B_HEADS)]
    st_refs = [sta_ref.at[h] for h in range(A_HEADS)] + [stb_ref.at[h] for h in range(B_HEADS)]

    for ci in range(CHUNKS_PER_STEP):
        rows = slice(ci * CHUNK, (ci + 1) * CHUNK)
        qa = qa_ref[rows, :]
        f = lb + (1.0 - lb) * _sigmoid(fa_ref[rows, :])
        z2 = (_dot(gkl_ref[rows, :].astype(BF16), wgk_ref[...]) + bgk_ref[...]) * LOG2E
        gk = (jnp.minimum(z2, 0.0) - jnp.log2(1.0 + jnp.exp2(-jnp.abs(z2)))) * (1.0 / GATE_LOGIT_NORMALIZER)

        q = jnp.concatenate([qa * _sigmoid(qa), qb_ref[rows, :] * (B_KEY_DIM ** -0.5)], axis=1)
        k = jnp.concatenate([1.0 - f, kb_ref[rows, :]], axis=1)
        g2 = jnp.concatenate([jnp.log2(f), gk], axis=1)
        v_heads = [ia_ref[rows, sl] for sl in a_cols] + [vb_ref[rows, sl] for sl in b_cols]
        outs = _chunk_attend(q, k, g2, v_heads, st_refs)

        for sl, o in zip(a_cols, outs[:A_HEADS]):
            ya_ref[rows, sl] = _gated_head_norm(o, ga_ref[rows, sl], anw).astype(ya_ref.dtype)
        for sl, o in zip(b_cols, outs[A_HEADS:]):
            yb_ref[rows, sl] = _gated_head_norm(o, gb_ref[rows, sl], bnw).astype(yb_ref.dtype)


def _mixers(proj, gk_low, lb_param, w_gk2p, b_gk2, a_norm_w, b_norm_w, merge_weights, bsz, seq):
    m = proj.shape[0]
    tm = CHUNK * CHUNKS_PER_STEP
    nc = seq // tm
    steps = bsz * nc
    assert A_HEAD_DIM == B_KEY_DIM
    assert all(w.shape[0] % (steps * 16) == 0 for w in merge_weights)

    def share(w):
        return pl.BlockSpec((w.shape[0] // steps, w.shape[1]), lambda b, c: (b * nc + c, 0))

    def col(c0, width):
        assert c0 % width == 0
        return pl.BlockSpec((tm, width), lambda b, c: (b * nc + c, c0 // width))

    const = lambda b, c: (0, 0)
    return pl.pallas_call(
        _mixers_kernel,
        grid=(bsz, nc),
        in_specs=[
            pl.BlockSpec((pl.Element(tm), pl.Element(MAIN_WIDTH - COL_QA)),
                         lambda b, c: (pl.multiple_of((b * nc + c) * tm, tm), COL_QA)),
            col(0, LANES),
            pl.BlockSpec(lb_param.shape, const),
            pl.BlockSpec((LANES, B_KEY_WIDTH), const),
            pl.BlockSpec((1, B_KEY_WIDTH), const),
            pl.BlockSpec((1, A_HEAD_DIM), const),
            pl.BlockSpec((1, B_VAL_DIM), const),
        ] + [share(w) for w in merge_weights],
        out_specs=[col(0, A_WIDTH), col(0, B_VAL_WIDTH)] + [share(w) for w in merge_weights],
        out_shape=[
            jax.ShapeDtypeStruct((m, A_WIDTH), BF16),
            jax.ShapeDtypeStruct((m, B_VAL_WIDTH), BF16),
        ] + [jax.ShapeDtypeStruct(w.shape, BF16) for w in merge_weights],
        scratch_shapes=[
            pltpu.VMEM((A_HEADS, A_HEAD_DIM, A_HEAD_DIM), F32),
            pltpu.VMEM((B_HEADS, B_VAL_DIM, B_KEY_DIM), F32),
        ],
        compiler_params=pltpu.CompilerParams(
            dimension_semantics=("arbitrary", "arbitrary"), vmem_limit_bytes=VMEM_LIMIT),
        name="mixers",
    )(proj, gk_low, lb_param, w_gk2p, b_gk2, a_norm_w, b_norm_w, *merge_weights)


def _merge_kernel(ya_ref, yb_ref, ma_ref, mb_ref, x_ref, mod_ref, nw_ref,
                  wua_ref, wub_ref, wo_ref, x1_ref, h2_ref):
    ua = _dot(ya_ref[...], wua_ref[...])
    ub = _dot(yb_ref[...], wub_ref[...])
    merged = _sigmoid(ma_ref[...]) * ua + _sigmoid(mb_ref[...]) * ub
    out = _dot(merged.astype(BF16), wo_ref[...])
    x1 = x_ref[...] + mod_ref[0, 2:3, :] * out
    x1_ref[...] = x1
    scale = nw_ref[...] * (1.0 + mod_ref[0, 4:5, :])
    h2_ref[...] = (_rms_rows(x1) * scale + mod_ref[0, 3:4, :]).astype(h2_ref.dtype)


def _merge(ya, yb, proj, x2, mod, norm_w, w_up_a, w_up_b, w_o, seq):
    m, d = x2.shape
    assert COL_MA % d == 0 and COL_MB % d == 0
    tiles_per_seq = seq // TM_MERGE
    row = lambda i: (i, 0)
    const = lambda i: (0, 0)
    return pl.pallas_call(
        _merge_kernel,
        grid=(m // TM_MERGE,),
        in_specs=[
            pl.BlockSpec((TM_MERGE, A_WIDTH), row),
            pl.BlockSpec((TM_MERGE, B_VAL_WIDTH), row),
            pl.BlockSpec((TM_MERGE, d), lambda i: (i, COL_MA // d)),
            pl.BlockSpec((TM_MERGE, d), lambda i: (i, COL_MB // d)),
            pl.BlockSpec((TM_MERGE, d), row),
            pl.BlockSpec((1, N_MOD, d), lambda i: (i // tiles_per_seq, 0, 0)),
            pl.BlockSpec((1, d), const),
            pl.BlockSpec(w_up_a.shape, const),
            pl.BlockSpec(w_up_b.shape, const),
            pl.BlockSpec(w_o.shape, const),
        ],
        out_specs=[pl.BlockSpec((TM_MERGE, d), row), pl.BlockSpec((TM_MERGE, d), row)],
        out_shape=[jax.ShapeDtypeStruct((m, d), F32), jax.ShapeDtypeStruct((m, d), BF16)],
        compiler_params=pltpu.CompilerParams(
            dimension_semantics=("arbitrary",), vmem_limit_bytes=VMEM_LIMIT),
        name="merge",
    )(ya, yb, proj, proj, x2, mod, norm_w, w_up_a, w_up_b, w_o)


def _ffn_up_kernel(h_ref, wg_ref, wu_ref, wdown_ref, o_ref, wdown_bf_ref):
    for r in range(0, o_ref.shape[0], FFN_UP_SUBROWS):
        rs = slice(r, r + FFN_UP_SUBROWS)
        h = h_ref[rs, :]
        for t in range(0, o_ref.shape[1], FFN_UP_SUBTILE):
            sl = slice(t, t + FFN_UP_SUBTILE)
            gate = _dot(h, wg_ref[:, sl].astype(BF16))
            up = _dot(h, wu_ref[:, sl].astype(BF16))
            o_ref[rs, sl] = (gate * _sigmoid(gate) * up).astype(o_ref.dtype)
    wdown_bf_ref[...] = wdown_ref[...].astype(wdown_bf_ref.dtype)


def _ffn_up(h2, w_in, w_down):
    m, d = h2.shape
    hidden = w_in.shape[1] // 2
    nf = hidden // TF_FFN
    steps = (m // TM_FFN_UP) * nf
    assert w_down.shape == (hidden, d) and hidden % (steps * 16) == 0
    share = hidden // steps
    return pl.pallas_call(
        _ffn_up_kernel,
        grid=(m // TM_FFN_UP, nf),
        in_specs=[
            pl.BlockSpec((TM_FFN_UP, d), lambda i, j: (i, 0)),
            pl.BlockSpec((d, TF_FFN), lambda i, j: (0, j)),
            pl.BlockSpec((d, TF_FFN), lambda i, j: (0, j + nf)),
            pl.BlockSpec((share, d), lambda i, j: (i * nf + j, 0)),
        ],
        out_specs=[
            pl.BlockSpec((TM_FFN_UP, TF_FFN), lambda i, j: (i, j)),
            pl.BlockSpec((share, d), lambda i, j: (i * nf + j, 0)),
        ],
        out_shape=[
            jax.ShapeDtypeStruct((m, hidden), BF16),
            jax.ShapeDtypeStruct((hidden, d), BF16),
        ],
        compiler_params=pltpu.CompilerParams(
            dimension_semantics=("arbitrary", "arbitrary"), vmem_limit_bytes=VMEM_LIMIT),
        name="ffn_up",
    )(h2, w_in, w_in, w_down)


def _ffn_down_kernel(a_ref, wo_ref, x1_ref, mod_ref, nw_ref, o_ref):
    x2 = x1_ref[...] + mod_ref[0, 5:6, :] * _dot(a_ref[...], wo_ref[...])
    o_ref[...] = _rms_rows(x2) * nw_ref[...]


def _ffn_down(act, w_out, x1, mod, norm_w, seq):
    m, d = x1.shape
    hidden = w_out.shape[0]
    tiles_per_seq = seq // TM_FFN_DOWN
    return pl.pallas_call(
        _ffn_down_kernel,
        grid=(m // TM_FFN_DOWN,),
        in_specs=[
            pl.BlockSpec((TM_FFN_DOWN, hidden), lambda i: (i, 0)),
            pl.BlockSpec((hidden, d), lambda i: (0, 0)),
            pl.BlockSpec((TM_FFN_DOWN, d), lambda i: (i, 0)),
            pl.BlockSpec((1, N_MOD, d), lambda i: (i // tiles_per_seq, 0, 0)),
            pl.BlockSpec((1, d), lambda i: (0, 0)),
        ],
        out_specs=pl.BlockSpec((TM_FFN_DOWN, d), lambda i: (i, 0)),
        out_shape=jax.ShapeDtypeStruct((m, d), F32),
        compiler_params=pltpu.CompilerParams(
            dimension_semantics=("arbitrary",), vmem_limit_bytes=VMEM_LIMIT),
        name="ffn_down",
    )(act, w_out, x1, mod, norm_w)


def kernel(x, c, w_ada, b_ada, norm1_w, w_in, w_gk2, b_gk2, lb_param, a_norm_w, b_norm_w,
           w_up_a, w_up_b, w_o, norm2_w, w_ffn_in, w_ffn_out, final_norm_w):
    bsz, seq, d = x.shape
    depth = w_in.shape[0]
    assert depth == 1 and d == D_MODEL and w_in.shape[2] == MAIN_WIDTH + GK_RANK
    assert seq % TM_PROJ == 0 and seq % (CHUNK * CHUNKS_PER_STEP) == 0 and seq % TM_MERGE == 0
    assert seq % TM_FFN_UP == 0 and seq % TM_FFN_DOWN == 0
    m = bsz * seq
    x2 = x.reshape(m, d)

    def layer0(w):
        return w.reshape(w.shape[1:])

    mod = _ada(c, layer0(w_ada), b_ada).reshape(bsz, N_MOD, d)

    w_in_t = jnp.swapaxes(layer0(w_in), 0, 1).astype(BF16)
    proj, gk_low = _inproj(x2, mod, norm1_w, w_in_t, seq)

    w_gk2p = jnp.pad(layer0(w_gk2).astype(BF16), ((0, LANES - GK_RANK), (0, 0)))
    ya, yb, wua, wub, wo = _mixers(proj, gk_low, lb_param, w_gk2p, b_gk2, a_norm_w, b_norm_w,
                                   (layer0(w_up_a), layer0(w_up_b), layer0(w_o)), bsz, seq)

    x1, h2 = _merge(ya, yb, proj, x2, mod, norm2_w, wua, wub, wo, seq)

    act, w_down = _ffn_up(h2, layer0(w_ffn_in), layer0(w_ffn_out))
    out = _ffn_down(act, w_down, x1, mod, final_norm_w.reshape(1, d), seq)
    return out.reshape(bsz, seq, d)
```

```python
import functools
import math

import jax
import jax.numpy as jnp
from jax import lax
from jax.experimental import pallas as pl
from jax.experimental.pallas import tpu as pltpu

F32 = jnp.float32
BF16 = jnp.bfloat16

D_MODEL = 2048
A_HEADS = 8
A_HEAD_DIM = 128
A_WIDTH = A_HEADS * A_HEAD_DIM
B_HEADS = 4
B_KEY_DIM = 128
B_VAL_DIM = 256
B_KEY_WIDTH = B_HEADS * B_KEY_DIM
B_VAL_WIDTH = B_HEADS * B_VAL_DIM
GK_RANK = 16
GATE_LOGIT_NORMALIZER = 16.0
N_MOD = 6
EPS = 1e-6
LOG2E = math.log2(math.e)

LANES = 128
SUBLANES = 8
VMEM_LIMIT = 56 * 1024 * 1024
VMEM_LIMIT_PROJ = 60 * 1024 * 1024

MAIN_WIDTH = 4 * A_WIDTH + 2 * B_KEY_WIDTH + 2 * B_VAL_WIDTH + 2 * D_MODEL
COL_MA, COL_MB = 0, D_MODEL
COL_QA = 2 * D_MODEL
COL_FA, COL_IA, COL_GA = COL_QA + A_WIDTH, COL_QA + 2 * A_WIDTH, COL_QA + 3 * A_WIDTH
COL_QB = COL_QA + 4 * A_WIDTH
COL_KB = COL_QB + B_KEY_WIDTH
COL_VB = COL_KB + B_KEY_WIDTH
COL_GB = COL_VB + B_VAL_WIDTH

CHUNK = 128
CHUNKS_PER_STEP = 2
TM_PROJ = 1024
TN_PROJ = 2048
TM_MERGE = 512
MERGE_TILE = 256
TM_FFN_UP = 2048
TF_FFN = 512
FFN_UP_SUBTILE = 256
FFN_UP_SUBROWS = 1024
TM_FFN_DOWN = 512
TN_ADA = 1024


def _sigmoid(x):
    return 1.0 / (1.0 + jnp.exp2(x * (-LOG2E)))


def _rms_rows(x):
    return x * lax.rsqrt(jnp.mean(x * x, axis=-1, keepdims=True) + EPS)


def _dot(a, b):
    return jnp.dot(a, b, preferred_element_type=F32)


def _dot_nt(a, b):
    return lax.dot_general(a, b, (((1,), (1,)), ((), ())), preferred_element_type=F32)


def _dot_tn(a, b):
    return lax.dot_general(a, b, (((0,), (0,)), ((), ())), preferred_element_type=F32)


def _ada_kernel(c_ref, w_ref, b_ref, o_ref):
    c = c_ref[...]
    cond = (c * _sigmoid(c)).astype(BF16)
    o_ref[...] = _dot(cond, w_ref[...].astype(BF16)) + b_ref[...]


def _ada(c, w, b):
    bsz, d = c.shape
    n = w.shape[1]
    return pl.pallas_call(
        _ada_kernel,
        grid=(n // TN_ADA,),
        in_specs=[
            pl.BlockSpec((bsz, d), lambda j: (0, 0)),
            pl.BlockSpec((d, TN_ADA), lambda j: (0, j)),
            pl.BlockSpec((1, TN_ADA), lambda j: (0, j)),
        ],
        out_specs=pl.BlockSpec((bsz, TN_ADA), lambda j: (0, j)),
        out_shape=jax.ShapeDtypeStruct((bsz, n), F32),
        compiler_params=pltpu.CompilerParams(
            dimension_semantics=("arbitrary",), vmem_limit_bytes=VMEM_LIMIT),
        name="ada",
    )(c, w, b)


def _inproj_kernel(first_cols, xa_ref, xb_ref, mod_ref, nw_ref, w_ref, wgk_ref, o_ref, gk_ref, h_scr):
    j = pl.program_id(1)
    half = xa_ref.shape[0]

    @pl.when(j == 0)
    def _():
        scale = nw_ref[...] * (1.0 + mod_ref[0, 1:2, :])
        for x_ref, rs in ((xa_ref, slice(0, half)), (xb_ref, slice(half, 2 * half))):
            hb = (_rms_rows(x_ref[...]) * scale + mod_ref[0, 0:1, :]).astype(BF16)
            h_scr[rs, :] = hb
            gk_ref[rs, :] = _dot_nt(hb, wgk_ref[...])
        o_ref[:, :first_cols] = _dot_nt(h_scr[...], w_ref[:first_cols, :])
        if first_cols < o_ref.shape[1]:
            o_ref[:, first_cols:] = jnp.zeros((o_ref.shape[0], o_ref.shape[1] - first_cols), o_ref.dtype)

    @pl.when(j > 0)
    def _():
        o_ref[...] = _dot_nt(h_scr[...], w_ref[...])


def _inproj(x2, mod, norm_w, w_t, seq):
    m, d = x2.shape
    n_mix = MAIN_WIDTH - 2 * d
    gate0 = n_mix + GK_RANK
    assert w_t.shape == (MAIN_WIDTH + GK_RANK, d) and (2 * d) % TN_PROJ == 0
    assert gate0 % 16 == 0
    ng = 2 * d // TN_PROJ
    nm = pl.cdiv(n_mix, TN_PROJ)
    first_cols = n_mix - (nm - 1) * TN_PROJ
    assert nm * TN_PROJ <= w_t.shape[0]
    steps = ng + nm
    assert steps >= 5
    n_row_tiles = m // TM_PROJ
    tiles_per_seq = seq // TM_PROJ
    half = TM_PROJ // 2

    def w_rows(i, j):
        t = j - 1
        rows = jnp.where(t < ng, gate0 + t * TN_PROJ, (t - ng) * TN_PROJ)
        return (pl.multiple_of(jnp.where(j == 0, (nm - 1) * TN_PROJ, rows), 16), 0)

    def out_cols(i, j):
        return (i, jnp.where(j == 0, steps - 1, j - 1))

    def x_half(which, early_step):
        return pl.BlockSpec(
            (half, d),
            lambda i, j: (2 * jnp.minimum(i + (j >= early_step).astype(jnp.int32), n_row_tiles - 1) + which, 0))

    return pl.pallas_call(
        functools.partial(_inproj_kernel, first_cols),
        grid=(n_row_tiles, steps),
        in_specs=[
            x_half(0, steps - 3),
            x_half(1, steps - 2),
            pl.BlockSpec((1, N_MOD, d), lambda i, j: (i // tiles_per_seq, 0, 0)),
            pl.BlockSpec((1, d), lambda i, j: (0, 0)),
            pl.BlockSpec((pl.Element(TN_PROJ), pl.Element(d)), w_rows),
            pl.BlockSpec((pl.Element(LANES), pl.Element(d)), lambda i, j: (n_mix, 0)),
        ],
        out_specs=[
            pl.BlockSpec((TM_PROJ, TN_PROJ), out_cols),
            pl.BlockSpec((TM_PROJ, LANES), lambda i, j: (i, 0)),
        ],
        out_shape=[
            jax.ShapeDtypeStruct((m, steps * TN_PROJ), F32),
            jax.ShapeDtypeStruct((m, LANES), F32),
        ],
        scratch_shapes=[pltpu.VMEM((TM_PROJ, d), BF16)],
        compiler_params=pltpu.CompilerParams(
            dimension_semantics=("arbitrary", "arbitrary"), vmem_limit_bytes=VMEM_LIMIT_PROJ),
        name="inproj",
    )(x2, x2, mod, norm_w, w_t, w_t)


class _ScoreMasks:
    def __init__(self, c):
        sub = lax.broadcasted_iota(jnp.int32, (SUBLANES, c), 0)
        self.lane = lax.broadcasted_iota(jnp.int32, (SUBLANES, c), 1)
        in_tile = self.lane & (SUBLANES - 1)
        self.low_bit = sub ^ in_tile
        self.diag = jnp.where(in_tile < sub, self.lane >> int(math.log2(SUBLANES)), -1)

    def level(self, g):
        return (self.low_bit & -g) == g


def _group_mid_rows(p, g):
    c, dk = p.shape
    p3 = p.reshape(c // SUBLANES, SUBLANES, dk)
    sub = lax.broadcasted_iota(jnp.int32, p3.shape, 1)
    out = None
    for s in range(0, SUBLANES, 2 * g):
        part = jnp.broadcast_to(p3[:, s + g - 1:s + g, :], p3.shape)
        out = part if out is None else jnp.where(sub >= s, part, out)
    return out.reshape(c, dk)


def _chunk_attend(q, k, g2, v_heads, st_refs):
    c, width = q.shape
    heads = len(v_heads)
    dk = width // heads
    groups = c // SUBLANES
    hk = lambda a, h: a[:, h * dk:(h + 1) * dk]
    rows = lambda a, r: a[r * SUBLANES:(r + 1) * SUBLANES, :]
    row = lax.broadcasted_iota(jnp.int32, (c, width), 0)
    masks = _ScoreMasks(c)

    p = g2
    small = [[None] * groups for _ in range(heads)]
    g = c // 2
    plan = []
    while g >= SUBLANES:
        plan.append(g)
        g //= 2

    g = 1
    big_scores = {}
    while g < c:
        if g < SUBLANES:
            second = (row & g) != 0
            if g == 1:
                t = pltpu.roll(p, 1, axis=0)
                expo = jnp.where(second, p, 0.0)
            else:
                t = _group_mid_rows(p, g)
                expo = jnp.where(second, p, t - p)
            z = jnp.where(second, q, k) * jnp.exp2(expo)
            p = p + jnp.where(second, t, 0.0)
            zb = z.astype(BF16)
            lvl = None if g == 1 else masks.level(g)
            for h in range(heads):
                s = _dot_nt(hk(zb, h), hk(zb, h))
                for r in range(groups):
                    small[h][r] = rows(s, r) if g == 1 else jnp.where(lvl, rows(s, r), small[h][r])
        else:
            zs, ps = [], []
            for s0 in range(0, c, 2 * g):
                p1, p2 = p[s0:s0 + g, :], p[s0 + g:s0 + 2 * g, :]
                t = jnp.broadcast_to(p1[g - 1:g, :], (g, width))
                zs += [k[s0:s0 + g, :] * jnp.exp2(t - p1), q[s0 + g:s0 + 2 * g, :] * jnp.exp2(p2)]
                ps += [p1, p2 + t]
            zb = jnp.concatenate(zs, axis=0).astype(BF16)
            p = jnp.concatenate(ps, axis=0)
            zq = jnp.concatenate([zb[s0 + g:s0 + 2 * g, :] for s0 in range(0, c, 2 * g)], axis=0)
            big_scores[g] = [_dot_nt(hk(zq, h), hk(zb, h)) for h in range(heads)]
        g *= 2


    att = []
    for h in range(heads):
        out_rows = []
        for r in range(groups):
            first_row = r * SUBLANES
            acc, edge = None, 0
            for g in plan:
                if first_row & g:
                    blk = (first_row // (2 * g)) * g + (first_row % g)
                    piece = big_scores[g][h][blk:blk + SUBLANES, :]
                    acc = piece if acc is None else jnp.where(masks.lane < edge, acc, piece)
                    edge += g
            if acc is None:
                acc = jnp.zeros((SUBLANES, c), F32)
            else:
                acc = jnp.where(masks.lane < edge, acc, 0.0)
            out_rows.append(jnp.where(masks.diag == r, small[h][r], acc))
        att.append(jnp.concatenate(out_rows, axis=0).astype(BF16))

    qk = q * k
    q_dec = (q * jnp.exp2(p)).astype(BF16)
    p_last = p[c - 1:c, :]
    k_dec = (k * jnp.exp2(p_last - p)).astype(BF16)
    st_scale = jnp.exp2(p_last)
    outs = []
    for h in range(heads):
        v = v_heads[h]
        vb = v.astype(BF16)
        st = st_refs[h][...]
        o = _dot(att[h], vb)
        o = o + jnp.sum(hk(qk, h), axis=-1, keepdims=True) * v
        o = o + _dot_nt(hk(q_dec, h), st.astype(BF16))
        st_refs[h][...] = st * hk(st_scale, h) + _dot_tn(vb, hk(k_dec, h))
        outs.append(o)
    return outs


def _gated_head_norm(o, gate, w):
    return _rms_rows(o) * w * (gate * _sigmoid(gate))


def _mixers_kernel(mix_ref, gkl_ref, lbp_ref, wgk_ref, bgk_ref, anw_ref, bnw_ref, wua_ref, wub_ref, wo_ref,
                   ya_ref, yb_ref, wua_bf_ref, wub_bf_ref, wo_bf_ref, sta_ref, stb_ref):
    group = lambda c0, width: mix_ref.at[:, c0 - COL_QA:c0 - COL_QA + width]
    qa_ref, fa_ref, ia_ref, ga_ref = (group(c0, A_WIDTH) for c0 in (COL_QA, COL_FA, COL_IA, COL_GA))
    qb_ref, kb_ref = group(COL_QB, B_KEY_WIDTH), group(COL_KB, B_KEY_WIDTH)
    vb_ref, gb_ref = group(COL_VB, B_VAL_WIDTH), group(COL_GB, B_VAL_WIDTH)

    @pl.when(pl.program_id(1) == 0)
    def _():
        sta_ref[...] = jnp.zeros_like(sta_ref)
        stb_ref[...] = jnp.zeros_like(stb_ref)

    for src, dst in ((wua_ref, wua_bf_ref), (wub_ref, wub_bf_ref), (wo_ref, wo_bf_ref)):
        dst[...] = src[...].astype(dst.dtype)

    lbp = lbp_ref[...]
    pe = jnp.exp(lbp - jnp.max(lbp, axis=0, keepdims=True))
    lb = pe[0:1, :] / jnp.sum(pe, axis=0, keepdims=True)
    anw, bnw = anw_ref[...], bnw_ref[...]
    a_cols = [slice(h * A_HEAD_DIM, (h + 1) * A_HEAD_DIM) for h in range(A_HEADS)]
    b_cols = [slice(h * B_VAL_DIM, (h + 1) * B_VAL_DIM) for h in range(B_HEADS)]
    st_refs = [sta_ref.at[h] for h in range(A_HEADS)] + [stb_ref.at[h] for h in range(B_HEADS)]

    for ci in range(CHUNKS_PER_STEP):
        rows = slice(ci * CHUNK, (ci + 1) * CHUNK)
        qa = qa_ref[rows, :]
        f = lb + (1.0 - lb) * _sigmoid(fa_ref[rows, :])
        z2 = (_dot(gkl_ref[rows, :].astype(BF16), wgk_ref[...]) + bgk_ref[...]) * LOG2E
        gk = (jnp.minimum(z2, 0.0) - jnp.log2(1.0 + jnp.exp2(-jnp.abs(z2)))) * (1.0 / GATE_LOGIT_NORMALIZER)

        q = jnp.concatenate([qa * _sigmoid(qa), qb_ref[rows, :] * (B_KEY_DIM ** -0.5)], axis=1)
        k = jnp.concatenate([1.0 - f, kb_ref[rows, :]], axis=1)
        g2 = jnp.concatenate([jnp.log2(f), gk], axis=1)
        v_heads = [ia_ref[rows, sl] for sl in a_cols] + [vb_ref[rows, sl] for sl in b_cols]
        outs = _chunk_attend(q, k, g2, v_heads, st_refs)

        for sl, o in zip(a_cols, outs[:A_HEADS]):
            ya_ref[rows, sl] = _gated_head_norm(o, ga_ref[rows, sl], anw).astype(ya_ref.dtype)
        for sl, o in zip(b_cols, outs[A_HEADS:]):
            yb_ref[rows, sl] = _gated_head_norm(o, gb_ref[rows, sl], bnw).astype(yb_ref.dtype)


def _mixers(proj, gk_low, lb_param, w_gk2p, b_gk2, a_norm_w, b_norm_w, merge_weights, bsz, seq):
    m = proj.shape[0]
    tm = CHUNK * CHUNKS_PER_STEP
    nc = seq // tm
    steps = bsz * nc
    assert A_HEAD_DIM == B_KEY_DIM
    assert all(w.shape[0] % (steps * 16) == 0 for w in merge_weights)

    def share(w):
        return pl.BlockSpec((w.shape[0] // steps, w.shape[1]), lambda b, c: (b * nc + c, 0))

    def col(c0, width):
        assert c0 % width == 0
        return pl.BlockSpec((tm, width), lambda b, c: (b * nc + c, c0 // width))

    const = lambda b, c: (0, 0)
    return pl.pallas_call(
        _mixers_kernel,
        grid=(bsz, nc),
        in_specs=[
            pl.BlockSpec((pl.Element(tm), pl.Element(MAIN_WIDTH - COL_QA)),
                         lambda b, c: (pl.multiple_of((b * nc + c) * tm, tm), COL_QA)),
            col(0, LANES),
            pl.BlockSpec(lb_param.shape, const),
            pl.BlockSpec((LANES, B_KEY_WIDTH), const),
            pl.BlockSpec((1, B_KEY_WIDTH), const),
            pl.BlockSpec((1, A_HEAD_DIM), const),
            pl.BlockSpec((1, B_VAL_DIM), const),
        ] + [share(w) for w in merge_weights],
        out_specs=[col(0, A_WIDTH), col(0, B_VAL_WIDTH)] + [share(w) for w in merge_weights],
        out_shape=[
            jax.ShapeDtypeStruct((m, A_WIDTH), BF16),
            jax.ShapeDtypeStruct((m, B_VAL_WIDTH), BF16),
        ] + [jax.ShapeDtypeStruct(w.shape, BF16) for w in merge_weights],
        scratch_shapes=[
            pltpu.VMEM((A_HEADS, A_HEAD_DIM, A_HEAD_DIM), F32),
            pltpu.VMEM((B_HEADS, B_VAL_DIM, B_KEY_DIM), F32),
        ],
        compiler_params=pltpu.CompilerParams(
            dimension_semantics=("arbitrary", "arbitrary"), vmem_limit_bytes=VMEM_LIMIT),
        name="mixers",
    )(proj, gk_low, lb_param, w_gk2p, b_gk2, a_norm_w, b_norm_w, *merge_weights)


def _merge_kernel(ya_ref, yb_ref, ma_ref, mb_ref, x_ref, mod_ref, nw_ref,
                  wua_ref, wub_ref, wo_ref, x1_ref, h2_ref, mg_scr):
    d = x_ref.shape[1]
    tiles = [slice(t, t + MERGE_TILE) for t in range(0, d, MERGE_TILE)]
    ya, yb = ya_ref[...], yb_ref[...]
    for sl in tiles:
        merged = (_sigmoid(ma_ref[:, sl]) * _dot(ya, wua_ref[:, sl])
                  + _sigmoid(mb_ref[:, sl]) * _dot(yb, wub_ref[:, sl]))
        mg_scr[:, sl] = merged.astype(mg_scr.dtype)
    mg = mg_scr[...]
    sumsq = None
    for sl in tiles:
        x1 = x_ref[:, sl] + mod_ref[0, 2:3, sl] * _dot(mg, wo_ref[:, sl])
        x1_ref[:, sl] = x1
        part = jnp.sum(x1 * x1, axis=-1, keepdims=True)
        sumsq = part if sumsq is None else sumsq + part
    inv = lax.rsqrt(sumsq * (1.0 / d) + EPS)
    scale = nw_ref[...] * (1.0 + mod_ref[0, 4:5, :])
    for sl in tiles:
        h2_ref[:, sl] = (x1_ref[:, sl] * inv * scale[:, sl] + mod_ref[0, 3:4, sl]).astype(h2_ref.dtype)


def _merge(ya, yb, proj, x2, mod, norm_w, w_up_a, w_up_b, w_o, seq):
    m, d = x2.shape
    assert COL_MA % d == 0 and COL_MB % d == 0
    tiles_per_seq = seq // TM_MERGE
    row = lambda i: (i, 0)
    const = lambda i: (0, 0)
    return pl.pallas_call(
        _merge_kernel,
        grid=(m // TM_MERGE,),
        in_specs=[
            pl.BlockSpec((TM_MERGE, A_WIDTH), row),
            pl.BlockSpec((TM_MERGE, B_VAL_WIDTH), row),
            pl.BlockSpec((TM_MERGE, d), lambda i: (i, COL_MA // d)),
            pl.BlockSpec((TM_MERGE, d), lambda i: (i, COL_MB // d)),
            pl.BlockSpec((TM_MERGE, d), row),
            pl.BlockSpec((1, N_MOD, d), lambda i: (i // tiles_per_seq, 0, 0)),
            pl.BlockSpec((1, d), const),
            pl.BlockSpec(w_up_a.shape, const),
            pl.BlockSpec(w_up_b.shape, const),
            pl.BlockSpec(w_o.shape, const),
        ],
        out_specs=[pl.BlockSpec((TM_MERGE, d), row), pl.BlockSpec((TM_MERGE, d), row)],
        out_shape=[jax.ShapeDtypeStruct((m, d), F32), jax.ShapeDtypeStruct((m, d), BF16)],
        scratch_shapes=[pltpu.VMEM((TM_MERGE, d), BF16)],
        compiler_params=pltpu.CompilerParams(
            dimension_semantics=("arbitrary",), vmem_limit_bytes=VMEM_LIMIT_PROJ),
        name="merge",
    )(ya, yb, proj, proj, x2, mod, norm_w, w_up_a, w_up_b, w_o)


def _ffn_up_kernel(h_ref, wg_ref, wu_ref, wdown_ref, o_ref, wdown_bf_ref):
    for r in range(0, o_ref.shape[0], FFN_UP_SUBROWS):
        rs = slice(r, r + FFN_UP_SUBROWS)
        h = h_ref[rs, :]
        for t in range(0, o_ref.shape[1], FFN_UP_SUBTILE):
            sl = slice(t, t + FFN_UP_SUBTILE)
            gate = _dot(h, wg_ref[:, sl].astype(BF16))
            up = _dot(h, wu_ref[:, sl].astype(BF16))
            o_ref[rs, sl] = (gate * _sigmoid(gate) * up).astype(o_ref.dtype)
    wdown_bf_ref[...] = wdown_ref[...].astype(wdown_bf_ref.dtype)


def _ffn_up(h2, w_in, w_down):
    m, d = h2.shape
    hidden = w_in.shape[1] // 2
    nf = hidden // TF_FFN
    steps = (m // TM_FFN_UP) * nf
    assert w_down.shape == (hidden, d) and hidden % (steps * 16) == 0
    share = hidden // steps
    return pl.pallas_call(
        _ffn_up_kernel,
        grid=(m // TM_FFN_UP, nf),
        in_specs=[
            pl.BlockSpec((TM_FFN_UP, d), lambda i, j: (i, 0)),
            pl.BlockSpec((d, TF_FFN), lambda i, j: (0, j)),
            pl.BlockSpec((d, TF_FFN), lambda i, j: (0, j + nf)),
            pl.BlockSpec((share, d), lambda i, j: (i * nf + j, 0)),
        ],
        out_specs=[
            pl.BlockSpec((TM_FFN_UP, TF_FFN), lambda i, j: (i, j)),
            pl.BlockSpec((share, d), lambda i, j: (i * nf + j, 0)),
        ],
        out_shape=[
            jax.ShapeDtypeStruct((m, hidden), BF16),
            jax.ShapeDtypeStruct((hidden, d), BF16),
        ],
        compiler_params=pltpu.CompilerParams(
            dimension_semantics=("arbitrary", "arbitrary"), vmem_limit_bytes=VMEM_LIMIT),
        name="ffn_up",
    )(h2, w_in, w_in, w_down)


def _ffn_down_kernel(a_ref, wo_ref, x1_ref, mod_ref, nw_ref, o_ref):
    x2 = x1_ref[...] + mod_ref[0, 5:6, :] * _dot(a_ref[...], wo_ref[...])
    o_ref[...] = _rms_rows(x2) * nw_ref[...]


def _ffn_down(act, w_out, x1, mod, norm_w, seq):
    m, d = x1.shape
    hidden = w_out.shape[0]
    tiles_per_seq = seq // TM_FFN_DOWN
    return pl.pallas_call(
        _ffn_down_kernel,
        grid=(m // TM_FFN_DOWN,),
        in_specs=[
            pl.BlockSpec((TM_FFN_DOWN, hidden), lambda i: (i, 0)),
            pl.BlockSpec((hidden, d), lambda i: (0, 0)),
            pl.BlockSpec((TM_FFN_DOWN, d), lambda i: (i, 0)),
            pl.BlockSpec((1, N_MOD, d), lambda i: (i // tiles_per_seq, 0, 0)),
            pl.BlockSpec((1, d), lambda i: (0, 0)),
        ],
        out_specs=pl.BlockSpec((TM_FFN_DOWN, d), lambda i: (i, 0)),
        out_shape=jax.ShapeDtypeStruct((m, d), F32),
        compiler_params=pltpu.CompilerParams(
            dimension_semantics=("arbitrary",), vmem_limit_bytes=VMEM_LIMIT),
        name="ffn_down",
    )(act, w_out, x1, mod, norm_w)


def kernel(x, c, w_ada, b_ada, norm1_w, w_in, w_gk2, b_gk2, lb_param, a_norm_w, b_norm_w,
           w_up_a, w_up_b, w_o, norm2_w, w_ffn_in, w_ffn_out, final_norm_w):
    bsz, seq, d = x.shape
    depth = w_in.shape[0]
    assert depth == 1 and d == D_MODEL and w_in.shape[2] == MAIN_WIDTH + GK_RANK
    assert seq % TM_PROJ == 0 and seq % (CHUNK * CHUNKS_PER_STEP) == 0 and seq % TM_MERGE == 0
    assert seq % TM_FFN_UP == 0 and seq % TM_FFN_DOWN == 0
    m = bsz * seq
    x2 = x.reshape(m, d)

    def layer0(w):
        return w.reshape(w.shape[1:])

    mod = _ada(c, layer0(w_ada), b_ada).reshape(bsz, N_MOD, d)

    w_in_t = jnp.swapaxes(layer0(w_in), 0, 1).astype(BF16)
    proj, gk_low = _inproj(x2, mod, norm1_w, w_in_t, seq)

    w_gk2p = jnp.pad(layer0(w_gk2).astype(BF16), ((0, LANES - GK_RANK), (0, 0)))
    ya, yb, wua, wub, wo = _mixers(proj, gk_low, lb_param, w_gk2p, b_gk2, a_norm_w, b_norm_w,
                                   (layer0(w_up_a), layer0(w_up_b), layer0(w_o)), bsz, seq)

    x1, h2 = _merge(ya, yb, proj, x2, mod, norm2_w, wua, wub, wo, seq)

    act, w_down = _ffn_up(h2, layer0(w_ffn_in), layer0(w_ffn_out))
    out = _ffn_down(act, w_down, x1, mod, final_norm_w.reshape(1, d), seq)
    return out.reshape(bsz, seq, d)
```

```python
import functools
import math

import jax
import jax.numpy as jnp
from jax import lax
from jax.experimental import pallas as pl
from jax.experimental.pallas import tpu as pltpu

F32 = jnp.float32
BF16 = jnp.bfloat16

D_MODEL = 2048
A_HEADS = 8
A_HEAD_DIM = 128
A_WIDTH = A_HEADS * A_HEAD_DIM
B_HEADS = 4
B_KEY_DIM = 128
B_VAL_DIM = 256
B_KEY_WIDTH = B_HEADS * B_KEY_DIM
B_VAL_WIDTH = B_HEADS * B_VAL_DIM
GK_RANK = 16
GATE_LOGIT_NORMALIZER = 16.0
N_MOD = 6
EPS = 1e-6
LOG2E = math.log2(math.e)

LANES = 128
SUBLANES = 8
BF16_SUBLANES = 2 * SUBLANES
VMEM_LIMIT = 56 * 1024 * 1024
VMEM_LIMIT_LARGE = 60 * 1024 * 1024

MAIN_WIDTH = 4 * A_WIDTH + 2 * B_KEY_WIDTH + 2 * B_VAL_WIDTH + 2 * D_MODEL
COL_MA, COL_MB = 0, D_MODEL
COL_QA = 2 * D_MODEL
COL_FA, COL_IA, COL_GA = COL_QA + A_WIDTH, COL_QA + 2 * A_WIDTH, COL_QA + 3 * A_WIDTH
COL_QB = COL_QA + 4 * A_WIDTH
COL_KB = COL_QB + B_KEY_WIDTH
COL_VB = COL_KB + B_KEY_WIDTH
COL_GB = COL_VB + B_VAL_WIDTH

CHUNK = 128
CHUNKS_PER_STEP = 2
TM_PROJ = 1024
TN_PROJ = 2048
TM_MERGE = 512
MERGE_TILE = 256
TM_FFN_UP = 2048
TF_FFN = 512
FFN_UP_SUBTILE = 256
FFN_UP_SUBROWS = 1024
TM_FFN_DOWN = 512
TN_ADA = 1024


def _sigmoid(x):
    return 1.0 / (1.0 + jnp.exp2(x * (-LOG2E)))


def _rms_rows(x):
    return x * lax.rsqrt(jnp.mean(x * x, axis=-1, keepdims=True) + EPS)


def _dot(a, b):
    return jnp.dot(a, b, preferred_element_type=F32)


def _dot_nt(a, b):
    return lax.dot_general(a, b, (((1,), (1,)), ((), ())), preferred_element_type=F32)


def _dot_tn(a, b):
    return lax.dot_general(a, b, (((0,), (0,)), ((), ())), preferred_element_type=F32)


def _ada_kernel(c_ref, w_ref, b_ref, o_ref):
    c = c_ref[...]
    cond = (c * _sigmoid(c)).astype(BF16)
    o_ref[...] = _dot(cond, w_ref[...].astype(BF16)) + b_ref[...]


def _ada(c, w, b):
    bsz, d = c.shape
    n = w.shape[1]
    return pl.pallas_call(
        _ada_kernel,
        grid=(n // TN_ADA,),
        in_specs=[
            pl.BlockSpec((bsz, d), lambda j: (0, 0)),
            pl.BlockSpec((d, TN_ADA), lambda j: (0, j)),
            pl.BlockSpec((1, TN_ADA), lambda j: (0, j)),
        ],
        out_specs=pl.BlockSpec((bsz, TN_ADA), lambda j: (0, j)),
        out_shape=jax.ShapeDtypeStruct((bsz, n), F32),
        compiler_params=pltpu.CompilerParams(
            dimension_semantics=("arbitrary",), vmem_limit_bytes=VMEM_LIMIT),
        name="ada",
    )(c, w, b)


def _inproj_kernel(first_cols, xa_ref, xb_ref, mod_ref, nw_ref, w_ref, wgk_ref, o_ref, gk_ref, h_scr):
    j = pl.program_id(1)
    half = xa_ref.shape[0]

    @pl.when(j == 0)
    def _():
        scale = nw_ref[...] * (1.0 + mod_ref[0, 1:2, :])
        for x_ref, rs in ((xa_ref, slice(0, half)), (xb_ref, slice(half, 2 * half))):
            hb = (_rms_rows(x_ref[...]) * scale + mod_ref[0, 0:1, :]).astype(BF16)
            h_scr[rs, :] = hb
            gk_ref[rs, :] = _dot_nt(hb, wgk_ref[...])
        o_ref[:, :first_cols] = _dot_nt(h_scr[...], w_ref[:first_cols, :])
        if first_cols < o_ref.shape[1]:
            o_ref[:, first_cols:] = jnp.zeros((o_ref.shape[0], o_ref.shape[1] - first_cols), o_ref.dtype)

    @pl.when(j > 0)
    def _():
        o_ref[...] = _dot_nt(h_scr[...], w_ref[...])


def _inproj(x2, mod, norm_w, w_t, seq):
    m, d = x2.shape
    n_mix = MAIN_WIDTH - 2 * d
    gate0 = n_mix + GK_RANK
    assert w_t.shape == (MAIN_WIDTH + GK_RANK, d) and (2 * d) % TN_PROJ == 0
    assert gate0 % BF16_SUBLANES == 0
    ng = 2 * d // TN_PROJ
    nm = pl.cdiv(n_mix, TN_PROJ)
    first_cols = n_mix - (nm - 1) * TN_PROJ
    assert nm * TN_PROJ <= w_t.shape[0]
    steps = ng + nm
    assert steps >= 5
    n_row_tiles = m // TM_PROJ
    tiles_per_seq = seq // TM_PROJ
    half = TM_PROJ // 2

    def w_rows(i, j):
        t = j - 1
        rows = jnp.where(t < ng, gate0 + t * TN_PROJ, (t - ng) * TN_PROJ)
        return (pl.multiple_of(jnp.where(j == 0, (nm - 1) * TN_PROJ, rows), BF16_SUBLANES), 0)

    def out_cols(i, j):
        return (i, jnp.where(j == 0, steps - 1, j - 1))

    def x_half(which, early_step):
        return pl.BlockSpec(
            (half, d),
            lambda i, j: (2 * jnp.minimum(i + (j >= early_step).astype(jnp.int32), n_row_tiles - 1) + which, 0))

    return pl.pallas_call(
        functools.partial(_inproj_kernel, first_cols),
        grid=(n_row_tiles, steps),
        in_specs=[
            x_half(0, steps - 3),
            x_half(1, steps - 2),
            pl.BlockSpec((1, N_MOD, d), lambda i, j: (i // tiles_per_seq, 0, 0)),
            pl.BlockSpec((1, d), lambda i, j: (0, 0)),
            pl.BlockSpec((pl.Element(TN_PROJ), pl.Element(d)), w_rows),
            pl.BlockSpec((pl.Element(LANES), pl.Element(d)), lambda i, j: (n_mix, 0)),
        ],
        out_specs=[
            pl.BlockSpec((TM_PROJ, TN_PROJ), out_cols),
            pl.BlockSpec((TM_PROJ, LANES), lambda i, j: (i, 0)),
        ],
        out_shape=[
            jax.ShapeDtypeStruct((m, steps * TN_PROJ), F32),
            jax.ShapeDtypeStruct((m, LANES), F32),
        ],
        scratch_shapes=[pltpu.VMEM((TM_PROJ, d), BF16)],
        compiler_params=pltpu.CompilerParams(
            dimension_semantics=("arbitrary", "arbitrary"), vmem_limit_bytes=VMEM_LIMIT_LARGE),
        name="inproj",
    )(x2, x2, mod, norm_w, w_t, w_t)


class _ScoreMasks:
    def __init__(self, c):
        sub = lax.broadcasted_iota(jnp.int32, (SUBLANES, c), 0)
        self.lane = lax.broadcasted_iota(jnp.int32, (SUBLANES, c), 1)
        in_tile = self.lane & (SUBLANES - 1)
        self.low_bit = sub ^ in_tile
        self.diag = jnp.where(in_tile < sub, self.lane >> int(math.log2(SUBLANES)), -1)

    def level(self, g):
        return (self.low_bit & -g) == g


def _group_mid_rows(p, g):
    c, dk = p.shape
    p3 = p.reshape(c // SUBLANES, SUBLANES, dk)
    sub = lax.broadcasted_iota(jnp.int32, p3.shape, 1)
    out = None
    for s in range(0, SUBLANES, 2 * g):
        part = jnp.broadcast_to(p3[:, s + g - 1:s + g, :], p3.shape)
        out = part if out is None else jnp.where(sub >= s, part, out)
    return out.reshape(c, dk)


def _chunk_attend(q, k, g2, v_heads, st_refs):
    c, width = q.shape
    heads = len(v_heads)
    dk = width // heads
    groups = c // SUBLANES
    hk = lambda a, h: a[:, h * dk:(h + 1) * dk]
    rows = lambda a, r: a[r * SUBLANES:(r + 1) * SUBLANES, :]
    row = lax.broadcasted_iota(jnp.int32, (c, width), 0)
    masks = _ScoreMasks(c)

    p = g2
    small = [[None] * groups for _ in range(heads)]
    g = c // 2
    plan = []
    while g >= SUBLANES:
        plan.append(g)
        g //= 2

    g = 1
    big_scores = {}
    while g < c:
        if g < SUBLANES:
            second = (row & g) != 0
            if g == 1:
                t = pltpu.roll(p, 1, axis=0)
                expo = jnp.where(second, p, 0.0)
            else:
                t = _group_mid_rows(p, g)
                expo = jnp.where(second, p, t - p)
            z = jnp.where(second, q, k) * jnp.exp2(expo)
            p = p + jnp.where(second, t, 0.0)
            zb = z.astype(BF16)
            lvl = None if g == 1 else masks.level(g)
            for h in range(heads):
                s = _dot_nt(hk(zb, h), hk(zb, h))
                for r in range(groups):
                    small[h][r] = rows(s, r) if g == 1 else jnp.where(lvl, rows(s, r), small[h][r])
        else:
            zs, ps = [], []
            for s0 in range(0, c, 2 * g):
                p1, p2 = p[s0:s0 + g, :], p[s0 + g:s0 + 2 * g, :]
                t = jnp.broadcast_to(p1[g - 1:g, :], (g, width))
                zs += [k[s0:s0 + g, :] * jnp.exp2(t - p1), q[s0 + g:s0 + 2 * g, :] * jnp.exp2(p2)]
                ps += [p1, p2 + t]
            zb = jnp.concatenate(zs, axis=0).astype(BF16)
            p = jnp.concatenate(ps, axis=0)
            zq = jnp.concatenate([zb[s0 + g:s0 + 2 * g, :] for s0 in range(0, c, 2 * g)], axis=0)
            big_scores[g] = [_dot_nt(hk(zq, h), hk(zb, h)) for h in range(heads)]
        g *= 2


    att = []
    for h in range(heads):
        out_rows = []
        for r in range(groups):
            first_row = r * SUBLANES
            acc, edge = None, 0
            for g in plan:
                if first_row & g:
                    blk = (first_row // (2 * g)) * g + (first_row % g)
                    piece = big_scores[g][h][blk:blk + SUBLANES, :]
                    acc = piece if acc is None else jnp.where(masks.lane < edge, acc, piece)
                    edge += g
            if acc is None:
                acc = jnp.zeros((SUBLANES, c), F32)
            else:
                acc = jnp.where(masks.lane < edge, acc, 0.0)
            out_rows.append(jnp.where(masks.diag == r, small[h][r], acc))
        att.append(jnp.concatenate(out_rows, axis=0).astype(BF16))

    qk = q * k
    q_dec = (q * jnp.exp2(p)).astype(BF16)
    p_last = p[c - 1:c, :]
    k_dec = (k * jnp.exp2(p_last - p)).astype(BF16)
    st_scale = jnp.exp2(p_last)
    outs = []
    for h in range(heads):
        v = v_heads[h]
        vb = v.astype(BF16)
        st = st_refs[h][...]
        o = _dot(att[h], vb)
        o = o + jnp.sum(hk(qk, h), axis=-1, keepdims=True) * v
        o = o + _dot_nt(hk(q_dec, h), st.astype(BF16))
        st_refs[h][...] = st * hk(st_scale, h) + _dot_tn(vb, hk(k_dec, h))
        outs.append(o)
    return outs


def _gated_head_norm(o, gate, w):
    return _rms_rows(o) * w * (gate * _sigmoid(gate))


def _mixers_kernel(mix_ref, gkl_ref, lbp_ref, wgk_ref, bgk_ref, anw_ref, bnw_ref, wua_ref, wub_ref, wo_ref,
                   ya_ref, yb_ref, wua_bf_ref, wub_bf_ref, wo_bf_ref, sta_ref, stb_ref):
    group = lambda c0, width: mix_ref.at[:, c0 - COL_QA:c0 - COL_QA + width]
    qa_ref, fa_ref, ia_ref, ga_ref = (group(c0, A_WIDTH) for c0 in (COL_QA, COL_FA, COL_IA, COL_GA))
    qb_ref, kb_ref = group(COL_QB, B_KEY_WIDTH), group(COL_KB, B_KEY_WIDTH)
    vb_ref, gb_ref = group(COL_VB, B_VAL_WIDTH), group(COL_GB, B_VAL_WIDTH)

    @pl.when(pl.program_id(1) == 0)
    def _():
        sta_ref[...] = jnp.zeros_like(sta_ref)
        stb_ref[...] = jnp.zeros_like(stb_ref)

    for src, dst in ((wua_ref, wua_bf_ref), (wub_ref, wub_bf_ref), (wo_ref, wo_bf_ref)):
        dst[...] = src[...].astype(dst.dtype)

    lbp = lbp_ref[...]
    pe = jnp.exp(lbp - jnp.max(lbp, axis=0, keepdims=True))
    lb = pe[0:1, :] / jnp.sum(pe, axis=0, keepdims=True)
    anw, bnw = anw_ref[...], bnw_ref[...]
    a_cols = [slice(h * A_HEAD_DIM, (h + 1) * A_HEAD_DIM) for h in range(A_HEADS)]
    b_cols = [slice(h * B_VAL_DIM, (h + 1) * B_VAL_DIM) for h in range(B_HEADS)]
    st_refs = [sta_ref.at[h] for h in range(A_HEADS)] + [stb_ref.at[h] for h in range(B_HEADS)]

    for ci in range(CHUNKS_PER_STEP):
        rows = slice(ci * CHUNK, (ci + 1) * CHUNK)
        qa = qa_ref[rows, :]
        f = lb + (1.0 - lb) * _sigmoid(fa_ref[rows, :])
        z2 = (_dot(gkl_ref[rows, :].astype(BF16), wgk_ref[...]) + bgk_ref[...]) * LOG2E
        gk = (jnp.minimum(z2, 0.0) - jnp.log2(1.0 + jnp.exp2(-jnp.abs(z2)))) * (1.0 / GATE_LOGIT_NORMALIZER)

        q = jnp.concatenate([qa * _sigmoid(qa), qb_ref[rows, :] * (B_KEY_DIM ** -0.5)], axis=1)
        k = jnp.concatenate([1.0 - f, kb_ref[rows, :]], axis=1)
        g2 = jnp.concatenate([jnp.log2(f), gk], axis=1)
        v_heads = [ia_ref[rows, sl] for sl in a_cols] + [vb_ref[rows, sl] for sl in b_cols]
        outs = _chunk_attend(q, k, g2, v_heads, st_refs)

        for sl, o in zip(a_cols, outs[:A_HEADS]):
            ya_ref[rows, sl] = _gated_head_norm(o, ga_ref[rows, sl], anw).astype(ya_ref.dtype)
        for sl, o in zip(b_cols, outs[A_HEADS:]):
            yb_ref[rows, sl] = _gated_head_norm(o, gb_ref[rows, sl], bnw).astype(yb_ref.dtype)


def _mixers(proj, gk_low, lb_param, w_gk2p, b_gk2, a_norm_w, b_norm_w, merge_weights, bsz, seq):
    m = proj.shape[0]
    tm = CHUNK * CHUNKS_PER_STEP
    nc = seq // tm
    steps = bsz * nc
    assert A_HEAD_DIM == B_KEY_DIM
    assert all(w.shape[0] % (steps * BF16_SUBLANES) == 0 for w in merge_weights)

    def share(w):
        return pl.BlockSpec((w.shape[0] // steps, w.shape[1]), lambda b, c: (b * nc + c, 0))

    def col(c0, width):
        assert c0 % width == 0
        return pl.BlockSpec((tm, width), lambda b, c: (b * nc + c, c0 // width))

    const = lambda b, c: (0, 0)
    return pl.pallas_call(
        _mixers_kernel,
        grid=(bsz, nc),
        in_specs=[
            pl.BlockSpec((pl.Element(tm), pl.Element(MAIN_WIDTH - COL_QA)),
                         lambda b, c: (pl.multiple_of((b * nc + c) * tm, tm), COL_QA)),
            col(0, LANES),
            pl.BlockSpec(lb_param.shape, const),
            pl.BlockSpec((LANES, B_KEY_WIDTH), const),
            pl.BlockSpec((1, B_KEY_WIDTH), const),
            pl.BlockSpec((1, A_HEAD_DIM), const),
            pl.BlockSpec((1, B_VAL_DIM), const),
        ] + [share(w) for w in merge_weights],
        out_specs=[col(0, A_WIDTH), col(0, B_VAL_WIDTH)] + [share(w) for w in merge_weights],
        out_shape=[
            jax.ShapeDtypeStruct((m, A_WIDTH), BF16),
            jax.ShapeDtypeStruct((m, B_VAL_WIDTH), BF16),
        ] + [jax.ShapeDtypeStruct(w.shape, BF16) for w in merge_weights],
        scratch_shapes=[
            pltpu.VMEM((A_HEADS, A_HEAD_DIM, A_HEAD_DIM), F32),
            pltpu.VMEM((B_HEADS, B_VAL_DIM, B_KEY_DIM), F32),
        ],
        compiler_params=pltpu.CompilerParams(
            dimension_semantics=("arbitrary", "arbitrary"), vmem_limit_bytes=VMEM_LIMIT),
        name="mixers",
    )(proj, gk_low, lb_param, w_gk2p, b_gk2, a_norm_w, b_norm_w, *merge_weights)


def _merge_kernel(ya_ref, yb_ref, ma_ref, mb_ref, x_ref, mod_ref, nw_ref,
                  wua_ref, wub_ref, wo_ref, x1_ref, h2_ref, mg_scr):
    d = x_ref.shape[1]
    tiles = [slice(t, t + MERGE_TILE) for t in range(0, d, MERGE_TILE)]
    ya, yb = ya_ref[...], yb_ref[...]
    for sl in tiles:
        merged = (_sigmoid(ma_ref[:, sl]) * _dot(ya, wua_ref[:, sl])
                  + _sigmoid(mb_ref[:, sl]) * _dot(yb, wub_ref[:, sl]))
        mg_scr[:, sl] = merged.astype(mg_scr.dtype)
    mg = mg_scr[...]
    sumsq = None
    for sl in tiles:
        x1 = x_ref[:, sl] + mod_ref[0, 2:3, sl] * _dot(mg, wo_ref[:, sl])
        x1_ref[:, sl] = x1
        part = jnp.sum(x1 * x1, axis=-1, keepdims=True)
        sumsq = part if sumsq is None else sumsq + part
    inv = lax.rsqrt(sumsq * (1.0 / d) + EPS)
    scale = nw_ref[...] * (1.0 + mod_ref[0, 4:5, :])
    for sl in tiles:
        h2_ref[:, sl] = (x1_ref[:, sl] * inv * scale[:, sl] + mod_ref[0, 3:4, sl]).astype(h2_ref.dtype)


def _merge(ya, yb, proj, x2, mod, norm_w, w_up_a, w_up_b, w_o, seq):
    m, d = x2.shape
    assert COL_MA % d == 0 and COL_MB % d == 0
    tiles_per_seq = seq // TM_MERGE
    row = lambda i: (i, 0)
    const = lambda i: (0, 0)
    return pl.pallas_call(
        _merge_kernel,
        grid=(m // TM_MERGE,),
        in_specs=[
            pl.BlockSpec((TM_MERGE, A_WIDTH), row),
            pl.BlockSpec((TM_MERGE, B_VAL_WIDTH), row),
            pl.BlockSpec((TM_MERGE, d), lambda i: (i, COL_MA // d)),
            pl.BlockSpec((TM_MERGE, d), lambda i: (i, COL_MB // d)),
            pl.BlockSpec((TM_MERGE, d), row),
            pl.BlockSpec((1, N_MOD, d), lambda i: (i // tiles_per_seq, 0, 0)),
            pl.BlockSpec((1, d), const),
            pl.BlockSpec(w_up_a.shape, const),
            pl.BlockSpec(w_up_b.shape, const),
            pl.BlockSpec(w_o.shape, const),
        ],
        out_specs=[pl.BlockSpec((TM_MERGE, d), row), pl.BlockSpec((TM_MERGE, d), row)],
        out_shape=[jax.ShapeDtypeStruct((m, d), F32), jax.ShapeDtypeStruct((m, d), BF16)],
        scratch_shapes=[pltpu.VMEM((TM_MERGE, d), BF16)],
        compiler_params=pltpu.CompilerParams(
            dimension_semantics=("arbitrary",), vmem_limit_bytes=VMEM_LIMIT_LARGE),
        name="merge",
    )(ya, yb, proj, proj, x2, mod, norm_w, w_up_a, w_up_b, w_o)


def _ffn_up_kernel(h_ref, wg_ref, wu_ref, wdown_ref, o_ref, wdown_bf_ref):
    for r in range(0, o_ref.shape[0], FFN_UP_SUBROWS):
        rs = slice(r, r + FFN_UP_SUBROWS)
        h = h_ref[rs, :]
        for t in range(0, o_ref.shape[1], FFN_UP_SUBTILE):
            sl = slice(t, t + FFN_UP_SUBTILE)
            gate = _dot(h, wg_ref[:, sl].astype(BF16))
            up = _dot(h, wu_ref[:, sl].astype(BF16))
            o_ref[rs, sl] = (gate * _sigmoid(gate) * up).astype(o_ref.dtype)
    wdown_bf_ref[...] = wdown_ref[...].astype(wdown_bf_ref.dtype)


def _ffn_up(h2, w_in, w_down):
    m, d = h2.shape
    hidden = w_in.shape[1] // 2
    nf = hidden // TF_FFN
    steps = (m // TM_FFN_UP) * nf
    assert w_down.shape == (hidden, d) and hidden % (steps * BF16_SUBLANES) == 0
    share = hidden // steps
    return pl.pallas_call(
        _ffn_up_kernel,
        grid=(m // TM_FFN_UP, nf),
        in_specs=[
            pl.BlockSpec((TM_FFN_UP, d), lambda i, j: (i, 0)),
            pl.BlockSpec((d, TF_FFN), lambda i, j: (0, j)),
            pl.BlockSpec((d, TF_FFN), lambda i, j: (0, j + nf)),
            pl.BlockSpec((share, d), lambda i, j: (i * nf + j, 0)),
        ],
        out_specs=[
            pl.BlockSpec((TM_FFN_UP, TF_FFN), lambda i, j: (i, j)),
            pl.BlockSpec((share, d), lambda i, j: (i * nf + j, 0)),
        ],
        out_shape=[
            jax.ShapeDtypeStruct((m, hidden), BF16),
            jax.ShapeDtypeStruct((hidden, d), BF16),
        ],
        compiler_params=pltpu.CompilerParams(
            dimension_semantics=("arbitrary", "arbitrary"), vmem_limit_bytes=VMEM_LIMIT),
        name="ffn_up",
    )(h2, w_in, w_in, w_down)


def _ffn_down_kernel(a_ref, wo_ref, x1_ref, mod_ref, nw_ref, o_ref):
    x2 = x1_ref[...] + mod_ref[0, 5:6, :] * _dot(a_ref[...], wo_ref[...])
    o_ref[...] = _rms_rows(x2) * nw_ref[...]


def _ffn_down(act, w_out, x1, mod, norm_w, seq):
    m, d = x1.shape
    hidden = w_out.shape[0]
    tiles_per_seq = seq // TM_FFN_DOWN
    return pl.pallas_call(
        _ffn_down_kernel,
        grid=(m // TM_FFN_DOWN,),
        in_specs=[
            pl.BlockSpec((TM_FFN_DOWN, hidden), lambda i: (i, 0)),
            pl.BlockSpec((hidden, d), lambda i: (0, 0)),
            pl.BlockSpec((TM_FFN_DOWN, d), lambda i: (i, 0)),
            pl.BlockSpec((1, N_MOD, d), lambda i: (i // tiles_per_seq, 0, 0)),
            pl.BlockSpec((1, d), lambda i: (0, 0)),
        ],
        out_specs=pl.BlockSpec((TM_FFN_DOWN, d), lambda i: (i, 0)),
        out_shape=jax.ShapeDtypeStruct((m, d), F32),
        compiler_params=pltpu.CompilerParams(
            dimension_semantics=("arbitrary",), vmem_limit_bytes=VMEM_LIMIT),
        name="ffn_down",
    )(act, w_out, x1, mod, norm_w)


def kernel(x, c, w_ada, b_ada, norm1_w, w_in, w_gk2, b_gk2, lb_param, a_norm_w, b_norm_w,
           w_up_a, w_up_b, w_o, norm2_w, w_ffn_in, w_ffn_out, final_norm_w):
    bsz, seq, d = x.shape
    depth = w_in.shape[0]
    assert depth == 1 and d == D_MODEL and w_in.shape[2] == MAIN_WIDTH + GK_RANK
    assert seq % TM_PROJ == 0 and seq % (CHUNK * CHUNKS_PER_STEP) == 0 and seq % TM_MERGE == 0
    assert seq % TM_FFN_UP == 0 and seq % TM_FFN_DOWN == 0
    m = bsz * seq
    x2 = x.reshape(m, d)

    def layer0(w):
        return w.reshape(w.shape[1:])

    mod = _ada(c, layer0(w_ada), b_ada).reshape(bsz, N_MOD, d)

    w_in_t = jnp.swapaxes(layer0(w_in), 0, 1).astype(BF16)
    proj, gk_low = _inproj(x2, mod, norm1_w, w_in_t, seq)

    w_gk2p = jnp.pad(layer0(w_gk2).astype(BF16), ((0, LANES - GK_RANK), (0, 0)))
    ya, yb, wua, wub, wo = _mixers(proj, gk_low, lb_param, w_gk2p, b_gk2, a_norm_w, b_norm_w,
                                   (layer0(w_up_a), layer0(w_up_b), layer0(w_o)), bsz, seq)

    x1, h2 = _merge(ya, yb, proj, x2, mod, norm2_w, wua, wub, wo, seq)

    act, w_down = _ffn_up(h2, layer0(w_ffn_in), layer0(w_ffn_out))
    out = _ffn_down(act, w_down, x1, mod, final_norm_w.reshape(1, d), seq)
    return out.reshape(bsz, seq, d)
```

```python
import functools
import math

import jax
import jax.numpy as jnp
from jax import lax
from jax.experimental import pallas as pl
from jax.experimental.pallas import tpu as pltpu

F32 = jnp.float32
BF16 = jnp.bfloat16

D_MODEL = 2048
A_HEADS = 8
A_HEAD_DIM = 128
A_WIDTH = A_HEADS * A_HEAD_DIM
B_HEADS = 4
B_KEY_DIM = 128
B_VAL_DIM = 256
B_KEY_WIDTH = B_HEADS * B_KEY_DIM
B_VAL_WIDTH = B_HEADS * B_VAL_DIM
GK_RANK = 16
GATE_LOGIT_NORMALIZER = 16.0
N_MOD = 6
EPS = 1e-6
LOG2E = math.log2(math.e)

LANES = 128
SUBLANES = 8
BF16_SUBLANES = 2 * SUBLANES
VMEM_LIMIT = 56 * 1024 * 1024
VMEM_LIMIT_LARGE = 60 * 1024 * 1024

MAIN_WIDTH = 4 * A_WIDTH + 2 * B_KEY_WIDTH + 2 * B_VAL_WIDTH + 2 * D_MODEL
COL_MA, COL_MB = 0, D_MODEL
COL_QA = 2 * D_MODEL
COL_FA, COL_IA, COL_GA = COL_QA + A_WIDTH, COL_QA + 2 * A_WIDTH, COL_QA + 3 * A_WIDTH
COL_QB = COL_QA + 4 * A_WIDTH
COL_KB = COL_QB + B_KEY_WIDTH
COL_VB = COL_KB + B_KEY_WIDTH
COL_GB = COL_VB + B_VAL_WIDTH

CHUNK = 128
CHUNKS_PER_STEP = 2
TM_PROJ = 1024
TN_PROJ = 2048
TM_MERGE = 512
MERGE_TILE = 256
TM_FFN_UP = 2048
TF_FFN = 512
FFN_UP_SUBTILE = 256
FFN_UP_SUBROWS = 1024
TM_FFN_DOWN = 512
TN_ADA = 1024


def _sigmoid(x):
    return 1.0 / (1.0 + jnp.exp2(x * (-LOG2E)))


def _rms_rows(x):
    return x * lax.rsqrt(jnp.mean(x * x, axis=-1, keepdims=True) + EPS)


def _dot(a, b):
    return jnp.dot(a, b, preferred_element_type=F32)


def _dot_nt(a, b):
    return lax.dot_general(a, b, (((1,), (1,)), ((), ())), preferred_element_type=F32)


def _dot_tn(a, b):
    return lax.dot_general(a, b, (((0,), (0,)), ((), ())), preferred_element_type=F32)


def _ada_kernel(c_ref, w_ref, b_ref, o_ref):
    c = c_ref[...]
    cond = (c * _sigmoid(c)).astype(BF16)
    o_ref[...] = _dot(cond, w_ref[...].astype(BF16)) + b_ref[...]


def _ada(c, w, b):
    bsz, d = c.shape
    n = w.shape[1]
    return pl.pallas_call(
        _ada_kernel,
        grid=(n // TN_ADA,),
        in_specs=[
            pl.BlockSpec((bsz, d), lambda j: (0, 0)),
            pl.BlockSpec((d, TN_ADA), lambda j: (0, j)),
            pl.BlockSpec((1, TN_ADA), lambda j: (0, j)),
        ],
        out_specs=pl.BlockSpec((bsz, TN_ADA), lambda j: (0, j)),
        out_shape=jax.ShapeDtypeStruct((bsz, n), F32),
        compiler_params=pltpu.CompilerParams(
            dimension_semantics=("arbitrary",), vmem_limit_bytes=VMEM_LIMIT),
        name="ada",
    )(c, w, b)


def _inproj_kernel(first_cols, xa_ref, xb_ref, mod_ref, nw_ref, w_ref, wgk_ref, o_ref, gk_ref, h_scr):
    j = pl.program_id(1)
    half = xa_ref.shape[0]

    @pl.when(j == 0)
    def _():
        scale = nw_ref[...] * (1.0 + mod_ref[0, 1:2, :])
        for x_ref, rs in ((xa_ref, slice(0, half)), (xb_ref, slice(half, 2 * half))):
            hb = (_rms_rows(x_ref[...]) * scale + mod_ref[0, 0:1, :]).astype(BF16)
            h_scr[rs, :] = hb
            gk_ref[rs, :] = _dot_nt(hb, wgk_ref[...])
        o_ref[:, :first_cols] = _dot_nt(h_scr[...], w_ref[:first_cols, :])
        if first_cols < o_ref.shape[1]:
            o_ref[:, first_cols:] = jnp.zeros((o_ref.shape[0], o_ref.shape[1] - first_cols), o_ref.dtype)

    @pl.when(j > 0)
    def _():
        o_ref[...] = _dot_nt(h_scr[...], w_ref[...])


def _inproj(x2, mod, norm_w, w_t, seq):
    m, d = x2.shape
    n_mix = MAIN_WIDTH - 2 * d
    gate0 = n_mix + GK_RANK
    assert w_t.shape == (MAIN_WIDTH + GK_RANK, d) and (2 * d) % TN_PROJ == 0
    assert gate0 % BF16_SUBLANES == 0
    ng = 2 * d // TN_PROJ
    nm = pl.cdiv(n_mix, TN_PROJ)
    first_cols = n_mix - (nm - 1) * TN_PROJ
    assert nm * TN_PROJ <= w_t.shape[0]
    steps = ng + nm
    assert steps >= 5
    n_row_tiles = m // TM_PROJ
    tiles_per_seq = seq // TM_PROJ
    half = TM_PROJ // 2

    def w_rows(i, j):
        t = j - 1
        rows = jnp.where(t < ng, gate0 + t * TN_PROJ, (t - ng) * TN_PROJ)
        return (pl.multiple_of(jnp.where(j == 0, (nm - 1) * TN_PROJ, rows), BF16_SUBLANES), 0)

    def out_cols(i, j):
        return (i, jnp.where(j == 0, steps - 1, j - 1))

    def x_half(which, early_step):
        return pl.BlockSpec(
            (half, d),
            lambda i, j: (2 * jnp.minimum(i + (j >= early_step).astype(jnp.int32), n_row_tiles - 1) + which, 0))

    return pl.pallas_call(
        functools.partial(_inproj_kernel, first_cols),
        grid=(n_row_tiles, steps),
        in_specs=[
            x_half(0, steps - 3),
            x_half(1, steps - 2),
            pl.BlockSpec((1, N_MOD, d), lambda i, j: (i // tiles_per_seq, 0, 0)),
            pl.BlockSpec((1, d), lambda i, j: (0, 0)),
            pl.BlockSpec((pl.Element(TN_PROJ), pl.Element(d)), w_rows),
            pl.BlockSpec((pl.Element(LANES), pl.Element(d)), lambda i, j: (n_mix, 0)),
        ],
        out_specs=[
            pl.BlockSpec((TM_PROJ, TN_PROJ), out_cols),
            pl.BlockSpec((TM_PROJ, LANES), lambda i, j: (i, 0)),
        ],
        out_shape=[
            jax.ShapeDtypeStruct((m, steps * TN_PROJ), F32),
            jax.ShapeDtypeStruct((m, LANES), F32),
        ],
        scratch_shapes=[pltpu.VMEM((TM_PROJ, d), BF16)],
        compiler_params=pltpu.CompilerParams(
            dimension_semantics=("arbitrary", "arbitrary"), vmem_limit_bytes=VMEM_LIMIT_LARGE),
        name="inproj",
    )(x2, x2, mod, norm_w, w_t, w_t)


class _ScoreMasks:
    def __init__(self, c):
        sub = lax.broadcasted_iota(jnp.int32, (SUBLANES, c), 0)
        self.lane = lax.broadcasted_iota(jnp.int32, (SUBLANES, c), 1)
        in_tile = self.lane & (SUBLANES - 1)
        self.low_bit = sub ^ in_tile
        self.diag = jnp.where(in_tile < sub, self.lane >> int(math.log2(SUBLANES)), -1)

    def level(self, g):
        return (self.low_bit & -g) == g


def _group_mid_rows(p, g):
    c, dk = p.shape
    p3 = p.reshape(c // SUBLANES, SUBLANES, dk)
    sub = lax.broadcasted_iota(jnp.int32, p3.shape, 1)
    out = None
    for s in range(0, SUBLANES, 2 * g):
        part = jnp.broadcast_to(p3[:, s + g - 1:s + g, :], p3.shape)
        out = part if out is None else jnp.where(sub >= s, part, out)
    return out.reshape(c, dk)


def _chunk_attend(q, k, g2, v_heads, st_refs):
    c, width = q.shape
    heads = len(v_heads)
    dk = width // heads
    groups = c // SUBLANES
    hk = lambda a, h: a[:, h * dk:(h + 1) * dk]
    rows = lambda a, r: a[r * SUBLANES:(r + 1) * SUBLANES, :]
    row = lax.broadcasted_iota(jnp.int32, (c, width), 0)
    masks = _ScoreMasks(c)

    p = g2
    small = [[None] * groups for _ in range(heads)]
    g = c // 2
    plan = []
    while g >= SUBLANES:
        plan.append(g)
        g //= 2

    g = 1
    big_scores = {}
    while g < c:
        if g < SUBLANES:
            second = (row & g) != 0
            if g == 1:
                t = pltpu.roll(p, 1, axis=0)
                expo = jnp.where(second, p, 0.0)
            else:
                t = _group_mid_rows(p, g)
                expo = jnp.where(second, p, t - p)
            z = jnp.where(second, q, k) * jnp.exp2(expo)
            p = p + jnp.where(second, t, 0.0)
            zb = z.astype(BF16)
            lvl = None if g == 1 else masks.level(g)
            for h in range(heads):
                s = _dot_nt(hk(zb, h), hk(zb, h))
                for r in range(groups):
                    small[h][r] = rows(s, r) if g == 1 else jnp.where(lvl, rows(s, r), small[h][r])
        else:
            zs, ps = [], []
            for s0 in range(0, c, 2 * g):
                p1, p2 = p[s0:s0 + g, :], p[s0 + g:s0 + 2 * g, :]
                t = jnp.broadcast_to(p1[g - 1:g, :], (g, width))
                zs += [k[s0:s0 + g, :] * jnp.exp2(t - p1), q[s0 + g:s0 + 2 * g, :] * jnp.exp2(p2)]
                ps += [p1, p2 + t]
            zb = jnp.concatenate(zs, axis=0).astype(BF16)
            p = jnp.concatenate(ps, axis=0)
            zq = jnp.concatenate([zb[s0 + g:s0 + 2 * g, :] for s0 in range(0, c, 2 * g)], axis=0)
            big_scores[g] = [_dot_nt(hk(zq, h), hk(zb, h)) for h in range(heads)]
        g *= 2


    att = []
    for h in range(heads):
        out_rows = []
        for r in range(groups):
            first_row = r * SUBLANES
            acc, edge = None, 0
            for g in plan:
                if first_row & g:
                    blk = (first_row // (2 * g)) * g + (first_row % g)
                    piece = big_scores[g][h][blk:blk + SUBLANES, :]
                    acc = piece if acc is None else jnp.where(masks.lane < edge, acc, piece)
                    edge += g
            if acc is None:
                acc = jnp.zeros((SUBLANES, c), F32)
            else:
                acc = jnp.where(masks.lane < edge, acc, 0.0)
            out_rows.append(jnp.where(masks.diag == r, small[h][r], acc))
        att.append(jnp.concatenate(out_rows, axis=0).astype(BF16))

    qk = q * k
    q_dec = (q * jnp.exp2(p)).astype(BF16)
    p_last = p[c - 1:c, :]
    k_dec = (k * jnp.exp2(p_last - p)).astype(BF16)
    st_scale = jnp.exp2(p_last)
    outs = []
    for h in range(heads):
        v = v_heads[h]
        vb = v.astype(BF16)
        st = st_refs[h][...]
        o = _dot(att[h], vb)
        o = o + jnp.sum(hk(qk, h), axis=-1, keepdims=True) * v
        o = o + _dot_nt(hk(q_dec, h), st.astype(BF16))
        st_refs[h][...] = st * hk(st_scale, h) + _dot_tn(vb, hk(k_dec, h))
        outs.append(o)
    return outs


def _gated_head_norm(o, gate, w):
    return _rms_rows(o) * w * (gate * _sigmoid(gate))


def _mixers_kernel(mix_ref, gkl_ref, lbp_ref, wgk_ref, bgk_ref, anw_ref, bnw_ref, wua_ref, wub_ref, wo_ref,
                   ya_ref, yb_ref, wua_bf_ref, wub_bf_ref, wo_bf_ref, sta_ref, stb_ref):
    group = lambda c0, width: mix_ref.at[:, c0 - COL_QA:c0 - COL_QA + width]
    qa_ref, fa_ref, ia_ref, ga_ref = (group(c0, A_WIDTH) for c0 in (COL_QA, COL_FA, COL_IA, COL_GA))
    qb_ref, kb_ref = group(COL_QB, B_KEY_WIDTH), group(COL_KB, B_KEY_WIDTH)
    vb_ref, gb_ref = group(COL_VB, B_VAL_WIDTH), group(COL_GB, B_VAL_WIDTH)

    @pl.when(pl.program_id(1) == 0)
    def _():
        sta_ref[...] = jnp.zeros_like(sta_ref)
        stb_ref[...] = jnp.zeros_like(stb_ref)

    for src, dst in ((wua_ref, wua_bf_ref), (wub_ref, wub_bf_ref), (wo_ref, wo_bf_ref)):
        dst[...] = src[...].astype(dst.dtype)

    lbp = lbp_ref[...]
    pe = jnp.exp(lbp - jnp.max(lbp, axis=0, keepdims=True))
    lb = pe[0:1, :] / jnp.sum(pe, axis=0, keepdims=True)
    anw, bnw = anw_ref[...], bnw_ref[...]
    a_cols = [slice(h * A_HEAD_DIM, (h + 1) * A_HEAD_DIM) for h in range(A_HEADS)]
    b_cols = [slice(h * B_VAL_DIM, (h + 1) * B_VAL_DIM) for h in range(B_HEADS)]
    st_refs = [sta_ref.at[h] for h in range(A_HEADS)] + [stb_ref.at[h] for h in range(B_HEADS)]

    for ci in range(CHUNKS_PER_STEP):
        rows = slice(ci * CHUNK, (ci + 1) * CHUNK)
        qa = qa_ref[rows, :]
        f = lb + (1.0 - lb) * _sigmoid(fa_ref[rows, :])
        z2 = (_dot(gkl_ref[rows, :].astype(BF16), wgk_ref[...]) + bgk_ref[...]) * LOG2E
        gk = (jnp.minimum(z2, 0.0) - jnp.log2(1.0 + jnp.exp2(-jnp.abs(z2)))) * (1.0 / GATE_LOGIT_NORMALIZER)

        q = jnp.concatenate([qa * _sigmoid(qa), qb_ref[rows, :] * (B_KEY_DIM ** -0.5)], axis=1)
        k = jnp.concatenate([1.0 - f, kb_ref[rows, :]], axis=1)
        g2 = jnp.concatenate([jnp.log2(f), gk], axis=1)
        v_heads = [ia_ref[rows, sl] for sl in a_cols] + [vb_ref[rows, sl] for sl in b_cols]
        outs = _chunk_attend(q, k, g2, v_heads, st_refs)

        for sl, o in zip(a_cols, outs[:A_HEADS]):
            ya_ref[rows, sl] = _gated_head_norm(o, ga_ref[rows, sl], anw).astype(ya_ref.dtype)
        for sl, o in zip(b_cols, outs[A_HEADS:]):
            yb_ref[rows, sl] = _gated_head_norm(o, gb_ref[rows, sl], bnw).astype(yb_ref.dtype)


def _mixers(proj, gk_low, lb_param, w_gk2p, b_gk2, a_norm_w, b_norm_w, merge_weights, bsz, seq):
    m = proj.shape[0]
    tm = CHUNK * CHUNKS_PER_STEP
    nc = seq // tm
    steps = bsz * nc
    assert A_HEAD_DIM == B_KEY_DIM
    assert all(w.shape[0] % (steps * BF16_SUBLANES) == 0 for w in merge_weights)

    def share(w):
        return pl.BlockSpec((w.shape[0] // steps, w.shape[1]), lambda b, c: (b * nc + c, 0))

    def col(c0, width):
        assert c0 % width == 0
        return pl.BlockSpec((tm, width), lambda b, c: (b * nc + c, c0 // width))

    const = lambda b, c: (0, 0)
    return pl.pallas_call(
        _mixers_kernel,
        grid=(bsz, nc),
        in_specs=[
            pl.BlockSpec((pl.Element(tm), pl.Element(MAIN_WIDTH - COL_QA)),
                         lambda b, c: (pl.multiple_of((b * nc + c) * tm, tm), COL_QA)),
            col(0, LANES),
            pl.BlockSpec(lb_param.shape, const),
            pl.BlockSpec((LANES, B_KEY_WIDTH), const),
            pl.BlockSpec((1, B_KEY_WIDTH), const),
            pl.BlockSpec((1, A_HEAD_DIM), const),
            pl.BlockSpec((1, B_VAL_DIM), const),
        ] + [share(w) for w in merge_weights],
        out_specs=[col(0, A_WIDTH), col(0, B_VAL_WIDTH)] + [share(w) for w in merge_weights],
        out_shape=[
            jax.ShapeDtypeStruct((m, A_WIDTH), BF16),
            jax.ShapeDtypeStruct((m, B_VAL_WIDTH), BF16),
        ] + [jax.ShapeDtypeStruct(w.shape, BF16) for w in merge_weights],
        scratch_shapes=[
            pltpu.VMEM((A_HEADS, A_HEAD_DIM, A_HEAD_DIM), F32),
            pltpu.VMEM((B_HEADS, B_VAL_DIM, B_KEY_DIM), F32),
        ],
        compiler_params=pltpu.CompilerParams(
            dimension_semantics=("arbitrary", "arbitrary"), vmem_limit_bytes=VMEM_LIMIT),
        name="mixers",
    )(proj, gk_low, lb_param, w_gk2p, b_gk2, a_norm_w, b_norm_w, *merge_weights)


def _merge_kernel(ya_ref, yb_ref, ma_ref, mb_ref, x_ref, mod_ref, nw_ref,
                  wua_ref, wub_ref, wo_ref, x1_ref, h2_ref, mg_scr):
    d = x_ref.shape[1]
    tiles = [slice(t, t + MERGE_TILE) for t in range(0, d, MERGE_TILE)]
    ya, yb = ya_ref[...], yb_ref[...]
    for sl in tiles:
        merged = (_sigmoid(ma_ref[:, sl]) * _dot(ya, wua_ref[:, sl])
                  + _sigmoid(mb_ref[:, sl]) * _dot(yb, wub_ref[:, sl]))
        mg_scr[:, sl] = merged.astype(mg_scr.dtype)
    mg = mg_scr[...]
    sumsq = None
    for sl in tiles:
        x1 = x_ref[:, sl] + mod_ref[0, 2:3, sl] * _dot(mg, wo_ref[:, sl])
        x1_ref[:, sl] = x1
        part = jnp.sum(x1 * x1, axis=-1, keepdims=True)
        sumsq = part if sumsq is None else sumsq + part
    inv = lax.rsqrt(sumsq * (1.0 / d) + EPS)
    scale = nw_ref[...] * (1.0 + mod_ref[0, 4:5, :])
    for sl in tiles:
        h2_ref[:, sl] = (x1_ref[:, sl] * inv * scale[:, sl] + mod_ref[0, 3:4, sl]).astype(h2_ref.dtype)


def _merge(ya, yb, proj, x2, mod, norm_w, w_up_a, w_up_b, w_o, seq):
    m, d = x2.shape
    assert COL_MA % d == 0 and COL_MB % d == 0
    tiles_per_seq = seq // TM_MERGE
    row = lambda i: (i, 0)
    const = lambda i: (0, 0)
    return pl.pallas_call(
        _merge_kernel,
        grid=(m // TM_MERGE,),
        in_specs=[
            pl.BlockSpec((TM_MERGE, A_WIDTH), row),
            pl.BlockSpec((TM_MERGE, B_VAL_WIDTH), row),
            pl.BlockSpec((TM_MERGE, d), lambda i: (i, COL_MA // d)),
            pl.BlockSpec((TM_MERGE, d), lambda i: (i, COL_MB // d)),
            pl.BlockSpec((TM_MERGE, d), row),
            pl.BlockSpec((1, N_MOD, d), lambda i: (i // tiles_per_seq, 0, 0)),
            pl.BlockSpec((1, d), const),
            pl.BlockSpec(w_up_a.shape, const),
            pl.BlockSpec(w_up_b.shape, const),
            pl.BlockSpec(w_o.shape, const),
        ],
        out_specs=[pl.BlockSpec((TM_MERGE, d), row), pl.BlockSpec((TM_MERGE, d), row)],
        out_shape=[jax.ShapeDtypeStruct((m, d), F32), jax.ShapeDtypeStruct((m, d), BF16)],
        scratch_shapes=[pltpu.VMEM((TM_MERGE, d), BF16)],
        compiler_params=pltpu.CompilerParams(
            dimension_semantics=("parallel",), vmem_limit_bytes=VMEM_LIMIT_LARGE),
        name="merge",
    )(ya, yb, proj, proj, x2, mod, norm_w, w_up_a, w_up_b, w_o)


def _ffn_up_kernel(h_ref, wg_ref, wu_ref, wdown_ref, o_ref, wdown_bf_ref):
    for r in range(0, o_ref.shape[0], FFN_UP_SUBROWS):
        rs = slice(r, r + FFN_UP_SUBROWS)
        h = h_ref[rs, :]
        for t in range(0, o_ref.shape[1], FFN_UP_SUBTILE):
            sl = slice(t, t + FFN_UP_SUBTILE)
            gate = _dot(h, wg_ref[:, sl].astype(BF16))
            up = _dot(h, wu_ref[:, sl].astype(BF16))
            o_ref[rs, sl] = (gate * _sigmoid(gate) * up).astype(o_ref.dtype)
    wdown_bf_ref[...] = wdown_ref[...].astype(wdown_bf_ref.dtype)


def _ffn_up(h2, w_in, w_down):
    m, d = h2.shape
    hidden = w_in.shape[1] // 2
    nf = hidden // TF_FFN
    steps = (m // TM_FFN_UP) * nf
    assert w_down.shape == (hidden, d) and hidden % (steps * BF16_SUBLANES) == 0
    share = hidden // steps
    return pl.pallas_call(
        _ffn_up_kernel,
        grid=(m // TM_FFN_UP, nf),
        in_specs=[
            pl.BlockSpec((TM_FFN_UP, d), lambda i, j: (i, 0)),
            pl.BlockSpec((d, TF_FFN), lambda i, j: (0, j)),
            pl.BlockSpec((d, TF_FFN), lambda i, j: (0, j + nf)),
            pl.BlockSpec((share, d), lambda i, j: (i * nf + j, 0)),
        ],
        out_specs=[
            pl.BlockSpec((TM_FFN_UP, TF_FFN), lambda i, j: (i, j)),
            pl.BlockSpec((share, d), lambda i, j: (i * nf + j, 0)),
        ],
        out_shape=[
            jax.ShapeDtypeStruct((m, hidden), BF16),
            jax.ShapeDtypeStruct((hidden, d), BF16),
        ],
        compiler_params=pltpu.CompilerParams(
            dimension_semantics=("parallel", "parallel"), vmem_limit_bytes=VMEM_LIMIT),
        name="ffn_up",
    )(h2, w_in, w_in, w_down)


def _ffn_down_kernel(a_ref, wo_ref, x1_ref, mod_ref, nw_ref, o_ref):
    x2 = x1_ref[...] + mod_ref[0, 5:6, :] * _dot(a_ref[...], wo_ref[...])
    o_ref[...] = _rms_rows(x2) * nw_ref[...]


def _ffn_down(act, w_out, x1, mod, norm_w, seq):
    m, d = x1.shape
    hidden = w_out.shape[0]
    tiles_per_seq = seq // TM_FFN_DOWN
    return pl.pallas_call(
        _ffn_down_kernel,
        grid=(m // TM_FFN_DOWN,),
        in_specs=[
            pl.BlockSpec((TM_FFN_DOWN, hidden), lambda i: (i, 0)),
            pl.BlockSpec((hidden, d), lambda i: (0, 0)),
            pl.BlockSpec((TM_FFN_DOWN, d), lambda i: (i, 0)),
            pl.BlockSpec((1, N_MOD, d), lambda i: (i // tiles_per_seq, 0, 0)),
            pl.BlockSpec((1, d), lambda i: (0, 0)),
        ],
        out_specs=pl.BlockSpec((TM_FFN_DOWN, d), lambda i: (i, 0)),
        out_shape=jax.ShapeDtypeStruct((m, d), F32),
        compiler_params=pltpu.CompilerParams(
            dimension_semantics=("parallel",), vmem_limit_bytes=VMEM_LIMIT),
        name="ffn_down",
    )(act, w_out, x1, mod, norm_w)


def kernel(x, c, w_ada, b_ada, norm1_w, w_in, w_gk2, b_gk2, lb_param, a_norm_w, b_norm_w,
           w_up_a, w_up_b, w_o, norm2_w, w_ffn_in, w_ffn_out, final_norm_w):
    bsz, seq, d = x.shape
    depth = w_in.shape[0]
    assert depth == 1 and d == D_MODEL and w_in.shape[2] == MAIN_WIDTH + GK_RANK
    assert seq % TM_PROJ == 0 and seq % (CHUNK * CHUNKS_PER_STEP) == 0 and seq % TM_MERGE == 0
    assert seq % TM_FFN_UP == 0 and seq % TM_FFN_DOWN == 0
    m = bsz * seq
    x2 = x.reshape(m, d)

    def layer0(w):
        return w.reshape(w.shape[1:])

    mod = _ada(c, layer0(w_ada), b_ada).reshape(bsz, N_MOD, d)

    w_in_t = jnp.swapaxes(layer0(w_in), 0, 1).astype(BF16)
    proj, gk_low = _inproj(x2, mod, norm1_w, w_in_t, seq)

    w_gk2p = jnp.pad(layer0(w_gk2).astype(BF16), ((0, LANES - GK_RANK), (0, 0)))
    ya, yb, wua, wub, wo = _mixers(proj, gk_low, lb_param, w_gk2p, b_gk2, a_norm_w, b_norm_w,
                                   (layer0(w_up_a), layer0(w_up_b), layer0(w_o)), bsz, seq)

    x1, h2 = _merge(ya, yb, proj, x2, mod, norm2_w, wua, wub, wo, seq)

    act, w_down = _ffn_up(h2, layer0(w_ffn_in), layer0(w_ffn_out))
    out = _ffn_down(act, w_down, x1, mod, final_norm_w.reshape(1, d), seq)
    return out.reshape(bsz, seq, d)
```
